```python
import functools
import jax, jax.numpy as jnp
from jax import lax
import numpy as np

D_MODEL = 1024
BATCH = 4
SEQ = 4096
DEPTH = 2
DEC_BATCH = 32
DEC_SEQ = 1
PAST_LEN = 8192
PAGE_SIZE = 128

N_HEADS = 8
HEAD_DIM = 64
D_ATTN = N_HEADS * HEAD_DIM
D_CONV = 512
CONV_W = 3
D_FF = -(-8 * D_MODEL // (3 * 256)) * 256
Q_BLOCK = 128
ALPHA = (2 * DEPTH) ** 0.25
BETA = (8 * DEPTH) ** -0.25
LN_EPS = 1e-5
SCALE = HEAD_DIM ** -0.5

OFF_Q = 0
OFF_K = OFF_Q + D_ATTN
OFF_V = OFF_K + D_ATTN
OFF_F = OFF_V + D_ATTN
OFF_H = OFF_F + N_HEADS
OFF_B = OFF_H + D_CONV
OFF_C = OFF_B + D_CONV
OFF_G = OFF_C + D_CONV
D_IN = OFF_G + 2 * D_MODEL

kernel_name = "fox_shortconv_gated_hybrid_step"


def layer_norm(x, g, b):
    xf = x.astype(jnp.float32)
    mu = jnp.mean(xf, axis=-1, keepdims=True)
    var = jnp.mean(jnp.square(xf - mu), axis=-1, keepdims=True)
    y = (xf - mu) * lax.rsqrt(var + LN_EPS) * g.astype(jnp.float32) + b.astype(jnp.float32)
    return y.astype(x.dtype)


def split_in(x, w_in, b_f, b_gate):
    bsz, s = x.shape[:2]
    z = jnp.einsum('bsd,de->bse', x, w_in)
    q = z[..., OFF_Q:OFF_K].reshape(bsz, s, N_HEADS, HEAD_DIM)
    k = z[..., OFF_K:OFF_V].reshape(bsz, s, N_HEADS, HEAD_DIM)
    v = z[..., OFF_V:OFF_F].reshape(bsz, s, N_HEADS, HEAD_DIM)
    logf = jax.nn.log_sigmoid((z[..., OFF_F:OFF_H] + b_f).astype(jnp.float32))
    h = z[..., OFF_H:OFF_B]
    gb = z[..., OFF_B:OFF_C]
    gc = z[..., OFF_C:OFF_G]
    gates = jax.nn.sigmoid(z[..., OFF_G:] + b_gate)
    return q, k, v, logf, h, gb, gc, gates


def fox_block(q_blk, c_q, pos_q, k, v, c_k, pos_k):
    logits = jnp.einsum('bthd,bshd->bhts', q_blk, k).astype(jnp.float32) * SCALE
    bias = jnp.transpose(c_q, (0, 2, 1))[:, :, :, None] - jnp.transpose(c_k, (0, 2, 1))[:, :, None, :]
    mask = pos_k[None, :] <= pos_q[:, None]
    logits = jnp.where(mask, logits + bias, -jnp.inf)
    p = jax.nn.softmax(logits, axis=-1)
    return jnp.einsum('bhts,bshd->bthd', p.astype(v.dtype), v)


def fox_prompt(q, k, v, logf):
    bsz, s = q.shape[:2]
    nb = s // Q_BLOCK
    c = jnp.cumsum(logf, axis=1)
    pos = jnp.arange(s)
    qb = jnp.transpose(q.reshape(bsz, nb, Q_BLOCK, N_HEADS, HEAD_DIM), (1, 0, 2, 3, 4))
    cb = jnp.transpose(c.reshape(bsz, nb, Q_BLOCK, N_HEADS), (1, 0, 2, 3))
    pb = pos.reshape(nb, Q_BLOCK)
    out = lax.map(lambda a: fox_block(a[0], a[1], a[2], k, v, c, pos), (qb, cb, pb))
    return jnp.transpose(out, (1, 0, 2, 3, 4)).reshape(bsz, s, D_ATTN)


def fox_sample(q, k_new, v_new, logf_new, cache_k_l, cache_v_l, cache_logf_l, page_table):
    db, ds = q.shape[:2]
    past = page_table.shape[1] * PAGE_SIZE
    k_past = cache_k_l[page_table].reshape(db, past, N_HEADS, HEAD_DIM)
    v_past = cache_v_l[page_table].reshape(db, past, N_HEADS, HEAD_DIM)
    lf_past = cache_logf_l[page_table].reshape(db, past, N_HEADS)
    k = jnp.concatenate([k_past, k_new.astype(k_past.dtype)], axis=1)
    v = jnp.concatenate([v_past, v_new.astype(v_past.dtype)], axis=1)
    c = jnp.cumsum(jnp.concatenate([lf_past.astype(jnp.float32), logf_new], axis=1), axis=1)
    pos = jnp.arange(past + ds)
    out = fox_block(q.astype(k.dtype), c[:, past:], pos[past:], k, v, c, pos)
    return out.reshape(db, ds, D_ATTN)


def short_conv(u, w, buf):
    s = u.shape[1]
    ue = jnp.concatenate([buf.astype(u.dtype), u], axis=1)
    out = w[0] * ue[:, 0:s]
    for j in range(1, CONV_W):
        out = out + w[j] * ue[:, j:j + s]
    return out, ue[:, -(CONV_W - 1):]


def trunk_layer(x, attn_fn, conv_buf, w_in, b_f, b_gate, conv_w, w_pa, w_pc, w_o,
                ln1_g, ln1_b, w_gu, w_down, ln2_g, ln2_b):
    q, k, v, logf, h, gb, gc, gates = split_in(x, w_in, b_f, b_gate)
    a = attn_fn(q, k, v, logf)
    cv, new_buf = short_conv(gc * h, conv_w, conv_buf)
    cb = gb * cv
    a_out = jnp.einsum('bse,ed->bsd', a.astype(x.dtype), w_pa)
    c_out = jnp.einsum('bse,ed->bsd', cb, w_pc)
    m = gates[..., :D_MODEL] * a_out + gates[..., D_MODEL:] * c_out
    tm = jnp.einsum('bsd,de->bse', m, w_o)
    x = layer_norm(ALPHA * x + tm, ln1_g, ln1_b)
    gu = jnp.einsum('bsd,df->bsf', x, w_gu)
    f = jnp.einsum('bsf,fd->bsd', jax.nn.silu(gu[..., :D_FF]) * gu[..., D_FF:], w_down)
    x = layer_norm(ALPHA * x + f, ln2_g, ln2_b)
    return x, k, v, logf, new_buf


def setup_inputs(seed: int = 0) -> dict:
    key = jax.random.key(seed)
    ks = jax.random.split(key, 24)
    f32 = jnp.float32
    n_pages = PAST_LEN // PAGE_SIZE
    n_used = DEC_BATCH * n_pages
    n_pool = (5 * n_used) // 4
    nrm = lambda k, shape, s: jax.random.normal(k, shape, f32) * s

    x_prompt = nrm(ks[0], (BATCH, SEQ, D_MODEL), 1.0)
    x_sample = nrm(ks[1], (DEC_BATCH, DEC_SEQ, D_MODEL), 1.0)
    cache_k = nrm(ks[2], (DEPTH, n_pool, PAGE_SIZE, N_HEADS, HEAD_DIM), 1.0)
    cache_v = nrm(ks[3], (DEPTH, n_pool, PAGE_SIZE, N_HEADS, HEAD_DIM), BETA)
    cache_logf = jax.nn.log_sigmoid(
        jax.random.uniform(ks[4], (DEPTH, 1, 1, N_HEADS), f32, 1.0, 5.0)
        + nrm(ks[5], (DEPTH, n_pool, PAGE_SIZE, N_HEADS), 0.5))
    state_conv = nrm(ks[6], (DEPTH, DEC_BATCH, CONV_W - 1, D_CONV), 0.5)
    page_table = jax.random.permutation(ks[7], n_pool)[:n_used].reshape(DEC_BATCH, n_pages).astype(jnp.int32)

    col_scale = jnp.concatenate([
        jnp.ones((2 * D_ATTN,), f32),
        jnp.full((D_ATTN,), BETA, f32),
        jnp.full((N_HEADS,), 0.1, f32),
        jnp.full((D_CONV,), BETA, f32),
        jnp.ones((2 * D_CONV + 2 * D_MODEL,), f32)])
    w_in = nrm(ks[8], (DEPTH, D_MODEL, D_IN), D_MODEL ** -0.5) * col_scale
    b_f = jax.random.uniform(ks[9], (DEPTH, N_HEADS), f32, 1.0, 5.0)
    b_gate = nrm(ks[10], (DEPTH, 2 * D_MODEL), 0.1)
    conv_w = nrm(ks[11], (DEPTH, CONV_W, D_CONV), CONV_W ** -0.5)
    w_attn_proj = nrm(ks[12], (DEPTH, D_ATTN, D_MODEL), D_ATTN ** -0.5)
    w_conv_proj = nrm(ks[13], (DEPTH, D_CONV, D_MODEL), D_CONV ** -0.5)
    w_out = nrm(ks[14], (DEPTH, D_MODEL, D_MODEL), BETA * D_MODEL ** -0.5)
    ln1_g = 1.0 + nrm(ks[15], (DEPTH, D_MODEL), 0.02)
    ln1_b = nrm(ks[16], (DEPTH, D_MODEL), 0.02)
    w_gate_up = nrm(ks[17], (DEPTH, D_MODEL, 2 * D_FF), BETA * D_MODEL ** -0.5)
    w_down = nrm(ks[18], (DEPTH, D_FF, D_MODEL), BETA * D_FF ** -0.5)
    ln2_g = 1.0 + nrm(ks[19], (DEPTH, D_MODEL), 0.02)
    ln2_b = nrm(ks[20], (DEPTH, D_MODEL), 0.02)
    return {
        "x_prompt": x_prompt, "x_sample": x_sample,
        "cache_k": cache_k, "cache_v": cache_v, "cache_logf": cache_logf,
        "state_conv": state_conv, "page_table": page_table,
        "w_in": w_in, "b_f": b_f, "b_gate": b_gate, "conv_w": conv_w,
        "w_attn_proj": w_attn_proj, "w_conv_proj": w_conv_proj, "w_out": w_out,
        "ln1_g": ln1_g, "ln1_b": ln1_b, "w_gate_up": w_gate_up, "w_down": w_down,
        "ln2_g": ln2_g, "ln2_b": ln2_b,
    }


def reference(x_prompt, x_sample, cache_k, cache_v, cache_logf, state_conv, page_table,
              w_in, b_f, b_gate, conv_w, w_attn_proj, w_conv_proj, w_out,
              ln1_g, ln1_b, w_gate_up, w_down, ln2_g, ln2_b):
    xp = x_prompt
    xs = x_sample
    kp, vp, lp, cp = [], [], [], []
    ksm, vsm, lsm, csm = [], [], [], []
    zero_buf = jnp.zeros((x_prompt.shape[0], CONV_W - 1, D_CONV), x_prompt.dtype)
    for l in range(DEPTH):
        params = (w_in[l], b_f[l], b_gate[l], conv_w[l], w_attn_proj[l], w_conv_proj[l], w_out[l],
                  ln1_g[l], ln1_b[l], w_gate_up[l], w_down[l], ln2_g[l], ln2_b[l])
        xp, k1, v1, lf1, buf1 = trunk_layer(xp, fox_prompt, zero_buf, *params)
        attn_s = functools.partial(fox_sample, cache_k_l=cache_k[l], cache_v_l=cache_v[l],
                                   cache_logf_l=cache_logf[l], page_table=page_table)
        xs, k2, v2, lf2, buf2 = trunk_layer(xs, attn_s, state_conv[l], *params)
        kp.append(k1); vp.append(v1); lp.append(lf1); cp.append(buf1)
        ksm.append(k2); vsm.append(v2); lsm.append(lf2); csm.append(buf2)
    k_prompt = jnp.stack(kp)
    v_prompt = jnp.stack(vp)
    logf_prompt = jnp.stack(lp)
    conv_prompt = jnp.stack(cp)
    k_sample = jnp.stack(ksm)
    v_sample = jnp.stack(vsm)
    logf_sample = jnp.stack(lsm)
    conv_sample = jnp.stack(csm)
    return (xp, xs, k_prompt, v_prompt, logf_prompt, conv_prompt,
            k_sample, v_sample, logf_sample, conv_sample)
```

```python
import functools

import jax
import jax.numpy as jnp
from jax import lax
from jax.experimental import pallas as pl
from jax.experimental.pallas import tpu as pltpu

D_MODEL = 1024
N_HEADS = 8
HEAD_DIM = 64
D_ATTN = N_HEADS * HEAD_DIM
D_CONV = 512
CONV_W = 3
D_FF = 2816
PAGE_SIZE = 128
LN_EPS = 1e-5
SCALE = HEAD_DIM ** -0.5

OFF_Q = 0
OFF_K = OFF_Q + D_ATTN
OFF_V = OFF_K + D_ATTN
OFF_F = OFF_V + D_ATTN
OFF_H = OFF_F + N_HEADS
OFF_B = OFF_H + D_CONV
OFF_C = OFF_B + D_CONV
OFF_G = OFF_C + D_CONV

LANES = 128
SUBLANES = 8
VMEM_LIMIT = 56 * 1024 * 1024

PROJ_ROWS = 512
ATTN_TILE = 512
MIX_ROWS = 512
FFN_ROWS = 512
FFN_COLS = D_FF // 2
PAGES_PER_STEP = 8

BF16 = jnp.bfloat16
F32 = jnp.float32

_NT = (((1,), (1,)), ((), ()))


def _dot(a, b):
    return jnp.dot(a, b, preferred_element_type=F32)


def _dot_nt(a, b):
    return lax.dot_general(a, b, _NT, preferred_element_type=F32)


def _split3(x):
    hi = x.astype(BF16)
    r1 = x - hi.astype(F32)
    mid = r1.astype(BF16)
    lo = (r1 - mid.astype(F32)).astype(BF16)
    return hi, mid, lo


def _lane_cumsum(x, tri):
    rows = x.shape[0]
    parts = jnp.concatenate(_split3(x), axis=0)
    cs = _dot(parts, tri)
    return cs[:rows] + cs[rows:2 * rows] + cs[2 * rows:]


def _layer_norm(y, g, b):
    mu = jnp.mean(y, axis=-1, keepdims=True)
    d = y - mu
    var = jnp.mean(d * d, axis=-1, keepdims=True)
    return d * lax.rsqrt(var + LN_EPS) * g + b


def _proj_kernel(x_ref, wn_ref, wkv_ref, wf_ref, bf_ref, cw_ref, tri_ref,
                 q_ref, kt_ref, vt_ref, lf_ref, c_ref, cb_ref, st_ref,
                 hist_ref, carry_ref):
    si = pl.program_id(1)
    rows = x_ref.shape[0]

    @pl.when(si == 0)
    def _():
        hist_ref[...] = jnp.zeros_like(hist_ref)
        carry_ref[...] = jnp.zeros_like(carry_ref)

    xb = x_ref[...].astype(BF16)
    z = _dot(xb, wn_ref[...])
    q_ref[...] = (z[:, 0:D_ATTN] * SCALE).astype(BF16)

    kv = _dot_nt(wkv_ref[...], xb)
    kt_ref[0] = kv[0:D_ATTN]
    vt_ref[0] = kv[D_ATTN:2 * D_ATTN]

    lf = jax.nn.log_sigmoid(_dot_nt(wf_ref[...], xb) + bf_ref[...])
    lf_ref[0] = lf
    carry = carry_ref[...]
    for j in range(rows // LANES):
        cs = _lane_cumsum(lf[:, j * LANES:(j + 1) * LANES], tri_ref[...]) + carry
        c_ref[0, :, j * LANES:(j + 1) * LANES] = cs
        carry = cs[:, LANES - 1:LANES]
    carry_ref[...] = carry

    h = z[:, D_ATTN:D_ATTN + D_CONV]
    gb = z[:, D_ATTN + D_CONV:D_ATTN + 2 * D_CONV]
    gc = z[:, D_ATTN + 2 * D_CONV:D_ATTN + 3 * D_CONV]
    u = gc * h
    hist = hist_ref[...]
    row = lax.broadcasted_iota(jnp.int32, (SUBLANES, D_CONV), 0)
    r1 = pltpu.roll(u, 1, 0)
    r2 = pltpu.roll(u, 2, 0)
    h1 = pltpu.roll(hist, 1, 0)
    h2 = pltpu.roll(hist, 2, 0)
    u1 = jnp.concatenate([jnp.where(row < 1, h1, r1[0:SUBLANES]), r1[SUBLANES:]], axis=0)
    u2 = jnp.concatenate([jnp.where(row < 2, h2, r2[0:SUBLANES]), r2[SUBLANES:]], axis=0)
    cw = cw_ref[...]
    cv = cw[0:1] * u2 + cw[1:2] * u1 + cw[2:3] * u
    cb_ref[...] = (gb * cv).astype(BF16)
    tail = u[rows - SUBLANES:rows]
    hist_ref[...] = tail

    @pl.when(si == pl.num_programs(1) - 1)
    def _():
        st_ref[0] = tail[SUBLANES - (CONV_W - 1):SUBLANES]


def _prompt_proj(x, wn, wkvt, wft, bf, cw, tri, batch, seq):
    rows = PROJ_ROWS
    ns = seq // rows
    row_map = lambda b, s: (b * ns + s, 0)
    const = lambda b, s: (0, 0)
    seq_map = lambda b, s: (b, 0, s)
    return pl.pallas_call(
        _proj_kernel,
        grid=(batch, ns),
        in_specs=[
            pl.BlockSpec((rows, D_MODEL), row_map),
            pl.BlockSpec(wn.shape, const),
            pl.BlockSpec(wkvt.shape, const),
            pl.BlockSpec(wft.shape, const),
            pl.BlockSpec(bf.shape, const),
            pl.BlockSpec(cw.shape, const),
            pl.BlockSpec(tri.shape, const),
        ],
        out_specs=[
            pl.BlockSpec((rows, D_ATTN), row_map),
            pl.BlockSpec((1, D_ATTN, rows), seq_map),
            pl.BlockSpec((1, D_ATTN, rows), seq_map),
            pl.BlockSpec((1, N_HEADS, rows), seq_map),
            pl.BlockSpec((1, N_HEADS, rows), seq_map),
            pl.BlockSpec((rows, D_CONV), row_map),
            pl.BlockSpec((1, CONV_W - 1, D_CONV), lambda b, s: (b, 0, 0)),
        ],
        out_shape=[
            jax.ShapeDtypeStruct((batch * seq, D_ATTN), BF16),
            jax.ShapeDtypeStruct((batch, D_ATTN, seq), F32),
            jax.ShapeDtypeStruct((batch, D_ATTN, seq), F32),
            jax.ShapeDtypeStruct((batch, N_HEADS, seq), F32),
            jax.ShapeDtypeStruct((batch, N_HEADS, seq), F32),
            jax.ShapeDtypeStruct((batch * seq, D_CONV), BF16),
            jax.ShapeDtypeStruct((batch, CONV_W - 1, D_CONV), F32),
        ],
        scratch_shapes=[
            pltpu.VMEM((SUBLANES, D_CONV), F32),
            pltpu.VMEM((N_HEADS, 1), F32),
        ],
        compiler_params=pltpu.CompilerParams(
            dimension_semantics=("arbitrary", "arbitrary"), vmem_limit_bytes=VMEM_LIMIT),
        name="prompt_proj",
    )(x, wn, wkvt, wft, bf, cw, tri)


def _attn_kernel(q_ref, kt_ref, vt_ref, c_ref, o_ref, kb_ref, vb_ref):
    pair = pl.program_id(1)
    qi = pl.program_id(2)
    tq = q_ref.shape[0]
    tk = tq

    @pl.when(qi == 0)
    def _():
        kb_ref[...] = kt_ref[0].astype(BF16)
        vb_ref[...] = vt_ref[0].astype(BF16)

    q2 = q_ref[...]
    lane = lax.broadcasted_iota(jnp.int32, (1, LANES), 1)
    causal = (lax.broadcasted_iota(jnp.int32, (tq, tk), 1)
              <= lax.broadcasted_iota(jnp.int32, (tq, tk), 0))

    def step(j, carry, qm, crow, masked):
        m, l, acc = carry
        col = pl.multiple_of(j * tk, tk)
        s = _dot(qm, kb_ref[:, pl.ds(col, tk)])
        s = s - c_ref[0, pl.ds(crow, 1), pl.ds(col, tk)]
        if masked:
            s = jnp.where(causal, s, -jnp.inf)
        m_new = jnp.maximum(m, jnp.max(s, axis=-1, keepdims=True))
        alpha = jnp.exp(m - m_new)
        p = jnp.exp(s - m_new)
        l = alpha * l + jnp.sum(p, axis=-1, keepdims=True)
        pv = _dot_nt(p.astype(BF16), vb_ref[:, pl.ds(col, tk)])
        return m_new, l, alpha * acc + pv

    outs = []
    for hh in range(2):
        qm = jnp.where((lane // HEAD_DIM) == hh, q2, jnp.zeros_like(q2))
        crow = 2 * pair + hh
        init = (jnp.full((tq, 1), -jnp.inf, F32), jnp.zeros((tq, 1), F32),
                jnp.zeros((tq, LANES), F32))
        carry = lax.fori_loop(0, qi, functools.partial(step, qm=qm, crow=crow, masked=False), init)
        m, l, acc = step(qi, carry, qm, crow, True)
        outs.append(acc / l)
    o_ref[...] = jnp.where((lane // HEAD_DIM) == 0, outs[0], outs[1]).astype(BF16)


def _prompt_attn(q, kt, vt, c, batch, seq):
    t = ATTN_TILE
    nq = seq // t
    pairs = D_ATTN // LANES
    return pl.pallas_call(
        _attn_kernel,
        grid=(batch, pairs, nq),
        in_specs=[
            pl.BlockSpec((t, LANES), lambda b, p, i: (b * nq + i, p)),
            pl.BlockSpec((1, LANES, seq), lambda b, p, i: (b, p, 0)),
            pl.BlockSpec((1, LANES, seq), lambda b, p, i: (b, p, 0)),
            pl.BlockSpec((1, N_HEADS, seq), lambda b, p, i: (b, 0, 0)),
        ],
        out_specs=pl.BlockSpec((t, LANES), lambda b, p, i: (b * nq + i, p)),
        out_shape=jax.ShapeDtypeStruct((batch * seq, D_ATTN), BF16),
        scratch_shapes=[pltpu.VMEM((LANES, seq), BF16), pltpu.VMEM((LANES, seq), BF16)],
        compiler_params=pltpu.CompilerParams(
            dimension_semantics=("arbitrary", "arbitrary", "arbitrary"),
            vmem_limit_bytes=VMEM_LIMIT),
        name="prompt_attn",
    )(q, kt, vt, c)


def _mix_kernel(alpha, x_ref, a_ref, cb_ref, wg_ref, bg_ref, wpa_ref, wpc_ref, wo_ref,
                g_ref, b_ref, o_ref):
    x = x_ref[...]
    gates = jax.nn.sigmoid(_dot(x.astype(BF16), wg_ref[...]) + bg_ref[...])
    a_out = _dot(a_ref[...], wpa_ref[...])
    c_out = _dot(cb_ref[...], wpc_ref[...])
    m = gates[:, :D_MODEL] * a_out + gates[:, D_MODEL:] * c_out
    tm = _dot(m.astype(BF16), wo_ref[...])
    o_ref[...] = _layer_norm(alpha * x + tm, g_ref[...], b_ref[...])


def _mix(x, a, cb, wg, bg, wpa, wpc, wo, g, b, alpha, rows):
    n = x.shape[0]
    row_map = lambda i: (i, 0)
    const = lambda i: (0, 0)
    return pl.pallas_call(
        functools.partial(_mix_kernel, alpha),
        grid=(n // rows,),
        in_specs=[
            pl.BlockSpec((rows, D_MODEL), row_map),
            pl.BlockSpec((rows, D_ATTN), row_map),
            pl.BlockSpec((rows, D_CONV), row_map),
            pl.BlockSpec(wg.shape, const),
            pl.BlockSpec(bg.shape, const),
            pl.BlockSpec(wpa.shape, const),
            pl.BlockSpec(wpc.shape, const),
            pl.BlockSpec(wo.shape, const),
            pl.BlockSpec(g.shape, const),
            pl.BlockSpec(b.shape, const),
        ],
        out_specs=pl.BlockSpec((rows, D_MODEL), row_map),
        out_shape=jax.ShapeDtypeStruct((n, D_MODEL), F32),
        compiler_params=pltpu.CompilerParams(
            dimension_semantics=("arbitrary",), vmem_limit_bytes=VMEM_LIMIT),
        name="mix",
    )(x, a, cb, wg, bg, wpa, wpc, wo, g, b)


def _ffn_kernel(alpha, x_ref, wg_ref, wu_ref, wd_ref, g_ref, b_ref, o_ref, acc_ref):
    f = pl.program_id(1)
    xb = x_ref[...].astype(BF16)
    act = jax.nn.silu(_dot(xb, wg_ref[...])) * _dot(xb, wu_ref[...])
    part = _dot(act.astype(BF16), wd_ref[...])

    @pl.when(f == 0)
    def _():
        acc_ref[...] = part

    @pl.when(f > 0)
    def _():
        acc_ref[...] += part

    @pl.when(f == pl.num_programs(1) - 1)
    def _():
        o_ref[...] = _layer_norm(alpha * x_ref[...] + acc_ref[...], g_ref[...], b_ref[...])


def _ffn(x, wg, wu, wd, g, b, alpha, rows):
    n = x.shape[0]
    nf = D_FF // FFN_COLS
    return pl.pallas_call(
        functools.partial(_ffn_kernel, alpha),
        grid=(n // rows, nf),
        in_specs=[
            pl.BlockSpec((rows, D_MODEL), lambda i, f: (i, 0)),
            pl.BlockSpec((D_MODEL, FFN_COLS), lambda i, f: (0, f)),
            pl.BlockSpec((D_MODEL, FFN_COLS), lambda i, f: (0, f)),
            pl.BlockSpec((FFN_COLS, D_MODEL), lambda i, f: (f, 0)),
            pl.BlockSpec(g.shape, lambda i, f: (0, 0)),
            pl.BlockSpec(b.shape, lambda i, f: (0, 0)),
        ],
        out_specs=pl.BlockSpec((rows, D_MODEL), lambda i, f: (i, 0)),
        out_shape=jax.ShapeDtypeStruct((n, D_MODEL), F32),
        scratch_shapes=[pltpu.VMEM((rows, D_MODEL), F32)],
        compiler_params=pltpu.CompilerParams(
            dimension_semantics=("arbitrary", "arbitrary"), vmem_limit_bytes=VMEM_LIMIT),
        name="ffn",
    )(x, wg, wu, wd, g, b)


def _sample_proj_kernel(x_ref, wn_ref, wkv_ref, wf_ref, bf_ref, cw_ref, h0_ref, h1_ref,
                        q_ref, k_ref, v_ref, lf_ref, cb_ref, u_ref):
    xb = x_ref[...].astype(BF16)
    z = _dot(xb, wn_ref[...])
    q_ref[...] = z[:, 0:D_ATTN] * SCALE
    kv = _dot_nt(xb, wkv_ref[...])
    k_ref[...] = kv[:, 0:D_ATTN]
    v_ref[...] = kv[:, D_ATTN:2 * D_ATTN]
    lf_ref[...] = jax.nn.log_sigmoid(_dot_nt(wf_ref[...], xb) + bf_ref[...])
    h = z[:, D_ATTN:D_ATTN + D_CONV]
    gb = z[:, D_ATTN + D_CONV:D_ATTN + 2 * D_CONV]
    gc = z[:, D_ATTN + 2 * D_CONV:D_ATTN + 3 * D_CONV]
    u = gc * h
    cw = cw_ref[...]
    cv = cw[0:1] * h0_ref[...] + cw[1:2] * h1_ref[...] + cw[2:3] * u
    cb_ref[...] = (gb * cv).astype(BF16)
    u_ref[...] = u


def _sample_proj(x, wn, wkvt, wft, bf, cw, h0, h1):
    n = x.shape[0]
    args = (x, wn, wkvt, wft, bf, cw, h0, h1)
    full = lambda a: pl.BlockSpec(a.shape, lambda i: (0,) * a.ndim)
    out_shape = [
        jax.ShapeDtypeStruct((n, D_ATTN), F32),
        jax.ShapeDtypeStruct((n, D_ATTN), F32),
        jax.ShapeDtypeStruct((n, D_ATTN), F32),
        jax.ShapeDtypeStruct((N_HEADS, n), F32),
        jax.ShapeDtypeStruct((n, D_CONV), BF16),
        jax.ShapeDtypeStruct((n, D_CONV), F32),
    ]
    return pl.pallas_call(
        _sample_proj_kernel,
        grid=(1,),
        in_specs=[full(a) for a in args],
        out_specs=[full(s) for s in out_shape],
        out_shape=out_shape,
        compiler_params=pltpu.CompilerParams(
            dimension_semantics=("arbitrary",), vmem_limit_bytes=VMEM_LIMIT),
        name="sample_proj",
    )(*args)


def _paged_attn_kernel(pages, pt_ref, q_ref, kn_ref, vn_ref, lfn_ref, tri_ref, *refs):
    k_refs = refs[0:pages]
    v_refs = refs[pages:2 * pages]
    lf_refs = refs[2 * pages:3 * pages]
    o_ref = refs[3 * pages]
    m_ref, l_ref, acc_ref, carry_ref = refs[3 * pages + 1:]
    g = pl.program_id(1)

    @pl.when(g == 0)
    def _():
        m_ref[...] = jnp.full_like(m_ref, -jnp.inf)
        l_ref[...] = jnp.zeros_like(l_ref)
        acc_ref[...] = jnp.zeros_like(acc_ref)
        carry_ref[...] = jnp.zeros_like(carry_ref)

    head = lax.broadcasted_iota(jnp.int32, (N_HEADS, D_ATTN), 0)
    col_head = lax.broadcasted_iota(jnp.int32, (N_HEADS, D_ATTN), 1) // HEAD_DIM
    diag = head == col_head
    qbd = jnp.where(diag, q_ref[0], 0.0).astype(BF16)

    m = m_ref[...]
    l = l_ref[...]
    acc = acc_ref[...]
    carry = carry_ref[...]
    for j in range(pages):
        s = _dot(qbd, k_refs[j][0, 0].astype(BF16))
        c = _lane_cumsum(lf_refs[j][0, 0], tri_ref[...]) + carry
        carry = c[:, LANES - 1:LANES]
        s = s - c
        m_new = jnp.maximum(m, jnp.max(s, axis=-1, keepdims=True))
        alpha = jnp.exp(m - m_new)
        p = jnp.exp(s - m_new)
        l = alpha * l + jnp.sum(p, axis=-1, keepdims=True)
        acc = alpha * acc + _dot_nt(p.astype(BF16), v_refs[j][0, 0].astype(BF16))
        m = m_new
    m_ref[...] = m
    l_ref[...] = l
    acc_ref[...] = acc
    carry_ref[...] = carry

    @pl.when(g == pl.num_programs(1) - 1)
    def _():
        kn = kn_ref[0].astype(BF16).astype(F32)
        vn = vn_ref[0].astype(BF16).astype(F32)
        s_new = jnp.sum(qbd.astype(F32) * kn, axis=-1, keepdims=True)
        s_new = s_new - (carry + lfn_ref[0])
        m_fin = jnp.maximum(m, s_new)
        alpha = jnp.exp(m - m_fin)
        p_new = jnp.exp(s_new - m_fin)
        l_fin = alpha * l + p_new
        acc_fin = alpha * acc + p_new.astype(BF16).astype(F32) * vn
        out = jnp.where(diag, acc_fin / l_fin, 0.0)
        o_ref[0] = jnp.sum(out, axis=0, keepdims=True)


def _paged_attn(layer, page_table, q, kn, vn, lfn, tri, cache_kt, cache_vt, cache_lft):
    n, n_pages = page_table.shape
    pages = PAGES_PER_STEP
    groups = n_pages // pages
    row = pl.BlockSpec((1, 1, D_ATTN), lambda b, g, pt: (b, 0, 0))

    def page_spec(shape, j):
        return pl.BlockSpec((1, 1) + shape,
                            lambda b, g, pt: (layer, pt[b, g * pages + j], 0, 0))

    in_specs = [row, row, row,
                pl.BlockSpec((1, N_HEADS, 1), lambda b, g, pt: (b, 0, 0)),
                pl.BlockSpec(tri.shape, lambda b, g, pt: (0, 0))]
    in_specs += [page_spec((D_ATTN, PAGE_SIZE), j) for j in range(pages)]
    in_specs += [page_spec((D_ATTN, PAGE_SIZE), j) for j in range(pages)]
    in_specs += [page_spec((N_HEADS, PAGE_SIZE), j) for j in range(pages)]
    grid_spec = pltpu.PrefetchScalarGridSpec(
        num_scalar_prefetch=1,
        grid=(n, groups),
        in_specs=in_specs,
        out_specs=row,
        scratch_shapes=[
            pltpu.VMEM((N_HEADS, 1), F32),
            pltpu.VMEM((N_HEADS, 1), F32),
            pltpu.VMEM((N_HEADS, D_ATTN), F32),
            pltpu.VMEM((N_HEADS, 1), F32),
        ],
    )
    return pl.pallas_call(
        functools.partial(_paged_attn_kernel, pages),
        grid_spec=grid_spec,
        out_shape=jax.ShapeDtypeStruct((n, 1, D_ATTN), F32),
        compiler_params=pltpu.CompilerParams(
            dimension_semantics=("arbitrary", "arbitrary"), vmem_limit_bytes=VMEM_LIMIT),
        name="paged_attn",
    )(page_table, q, kn, vn, lfn, tri,
      *([cache_kt] * pages), *([cache_vt] * pages), *([cache_lft] * pages))


def kernel(x_prompt, x_sample, cache_k, cache_v, cache_logf, state_conv, page_table, w_in, b_f, b_gate, conv_w, w_attn_proj, w_conv_proj, w_out, ln1_g, ln1_b, w_gate_up, w_down, ln2_g, ln2_b):
    batch, seq, _ = x_prompt.shape
    n_dec = x_sample.shape[0]
    depth = w_in.shape[0]
    n_pool = cache_k.shape[1]
    alpha = (2 * depth) ** 0.25

    cache_kt = jnp.transpose(cache_k, (0, 1, 3, 4, 2)).reshape(depth, n_pool, D_ATTN, PAGE_SIZE)
    cache_vt = jnp.transpose(cache_v, (0, 1, 3, 4, 2)).reshape(depth, n_pool, D_ATTN, PAGE_SIZE)
    cache_lft = jnp.transpose(cache_logf, (0, 1, 3, 2))
    tri = jnp.triu(jnp.ones((LANES, LANES), F32)).astype(BF16)

    xp = x_prompt.reshape(batch * seq, D_MODEL)
    xs = x_sample.reshape(n_dec, D_MODEL)
    kp, vp, lp, cp, ksm, vsm, lsm, csm = [], [], [], [], [], [], [], []
    for l in range(depth):
        wt = jnp.transpose(w_in[l])
        wn = jnp.concatenate([w_in[l][:, OFF_Q:OFF_K], w_in[l][:, OFF_H:OFF_G]], axis=1).astype(BF16)
        wkvt = wt[OFF_K:OFF_F].astype(BF16)
        wft = wt[OFF_F:OFF_H].astype(BF16)
        wg = w_in[l][:, OFF_G:].astype(BF16)
        bf = b_f[l].reshape(N_HEADS, 1)
        bg = b_gate[l].reshape(1, 2 * D_MODEL)
        cw = conv_w[l]
        wpa = w_attn_proj[l].astype(BF16)
        wpc = w_conv_proj[l].astype(BF16)
        wo = w_out[l].astype(BF16)
        wgf = w_gate_up[l][:, :D_FF].astype(BF16)
        wuf = w_gate_up[l][:, D_FF:].astype(BF16)
        wd = w_down[l].astype(BF16)
        g1, b1 = ln1_g[l].reshape(1, D_MODEL), ln1_b[l].reshape(1, D_MODEL)
        g2, b2 = ln2_g[l].reshape(1, D_MODEL), ln2_b[l].reshape(1, D_MODEL)

        q, kt, vt, lf, c, cb, st = _prompt_proj(xp, wn, wkvt, wft, bf, cw, tri, batch, seq)
        a = _prompt_attn(q, kt, vt, c, batch, seq)
        x1 = _mix(xp, a, cb, wg, bg, wpa, wpc, wo, g1, b1, alpha, MIX_ROWS)
        xp = _ffn(x1, wgf, wuf, wd, g2, b2, alpha, FFN_ROWS)
        kp.append(kt); vp.append(vt); lp.append(lf); cp.append(st)

        h0, h1 = state_conv[l, :, 0, :], state_conv[l, :, 1, :]
        qs, ks, vs, lfs, cbs, us = _sample_proj(xs, wn, wkvt, wft, bf, cw, h0, h1)
        a_s = _paged_attn(l, page_table, qs.reshape(n_dec, 1, D_ATTN), ks.reshape(n_dec, 1, D_ATTN),
                          vs.reshape(n_dec, 1, D_ATTN), jnp.transpose(lfs).reshape(n_dec, N_HEADS, 1),
                          tri, cache_kt, cache_vt, cache_lft)
        x1s = _mix(xs, a_s.reshape(n_dec, D_ATTN).astype(BF16), cbs, wg, bg, wpa, wpc, wo, g1, b1,
                   alpha, n_dec)
        xs = _ffn(x1s, wgf, wuf, wd, g2, b2, alpha, n_dec)
        ksm.append(ks); vsm.append(vs); lsm.append(lfs); csm.append(jnp.stack([h1, us], axis=1))

    def heads_last(t):
        return jnp.transpose(t.reshape(depth, batch, N_HEADS, HEAD_DIM, seq), (0, 1, 4, 2, 3))

    y_prompt = xp.reshape(batch, seq, D_MODEL)
    y_sample = xs.reshape(n_dec, 1, D_MODEL)
    k_prompt = heads_last(jnp.stack(kp))
    v_prompt = heads_last(jnp.stack(vp))
    logf_prompt = jnp.transpose(jnp.stack(lp), (0, 1, 3, 2))
    conv_prompt = jnp.stack(cp)
    k_sample = jnp.stack(ksm).reshape(depth, n_dec, 1, N_HEADS, HEAD_DIM)
    v_sample = jnp.stack(vsm).reshape(depth, n_dec, 1, N_HEADS, HEAD_DIM)
    logf_sample = jnp.transpose(jnp.stack(lsm), (0, 2, 1)).reshape(depth, n_dec, 1, N_HEADS)
    conv_sample = jnp.stack(csm)
    return (y_prompt, y_sample, k_prompt, v_prompt, logf_prompt, conv_prompt,
            k_sample, v_sample, logf_sample, conv_sample)
```

```python
import functools

import jax
import jax.numpy as jnp
from jax import lax
from jax.experimental import pallas as pl
from jax.experimental.pallas import tpu as pltpu

D_MODEL = 1024
N_HEADS = 8
HEAD_DIM = 64
D_ATTN = N_HEADS * HEAD_DIM
D_CONV = 512
CONV_W = 3
D_FF = 2816
PAGE_SIZE = 128
LN_EPS = 1e-5
SCALE = HEAD_DIM ** -0.5

OFF_Q = 0
OFF_K = OFF_Q + D_ATTN
OFF_V = OFF_K + D_ATTN
OFF_F = OFF_V + D_ATTN
OFF_H = OFF_F + N_HEADS
OFF_B = OFF_H + D_CONV
OFF_C = OFF_B + D_CONV
OFF_G = OFF_C + D_CONV

LANES = 128
SUBLANES = 8
VMEM_LIMIT = 56 * 1024 * 1024

PROJ_ROWS = 512
ATTN_TILE = 1024
MIX_ROWS = 512
FFN_ROWS = 512
FFN_COLS = D_FF // 2
PAGES_PER_STEP = 16

BF16 = jnp.bfloat16
F32 = jnp.float32

_NT = (((1,), (1,)), ((), ()))


def _dot(a, b):
    return jnp.dot(a, b, preferred_element_type=F32)


def _dot_nt(a, b):
    return lax.dot_general(a, b, _NT, preferred_element_type=F32)


def _split3(x):
    hi = x.astype(BF16)
    r1 = x - hi.astype(F32)
    mid = r1.astype(BF16)
    lo = (r1 - mid.astype(F32)).astype(BF16)
    return hi, mid, lo


def _lane_cumsum(x, tri):
    rows = x.shape[0]
    parts = jnp.concatenate(_split3(x), axis=0)
    cs = _dot(parts, tri)
    return cs[:rows] + cs[rows:2 * rows] + cs[2 * rows:]


def _layer_norm(y, g, b):
    mu = jnp.mean(y, axis=-1, keepdims=True)
    d = y - mu
    var = jnp.mean(d * d, axis=-1, keepdims=True)
    return d * lax.rsqrt(var + LN_EPS) * g + b


def _proj_kernel(x_ref, wn_ref, wkv_ref, wf_ref, bf_ref, cw_ref, tri_ref,
                 q_ref, kt_ref, vt_ref, lf_ref, c_ref, cb_ref, st_ref,
                 hist_ref, carry_ref):
    si = pl.program_id(1)
    rows = x_ref.shape[0]

    @pl.when(si == 0)
    def _():
        hist_ref[...] = jnp.zeros_like(hist_ref)
        carry_ref[...] = jnp.zeros_like(carry_ref)

    xb = x_ref[...].astype(BF16)
    z = _dot(xb, wn_ref[...])
    q_ref[...] = (z[:, 0:D_ATTN] * SCALE).astype(BF16)

    kv = _dot_nt(wkv_ref[...], xb)
    kt_ref[0] = kv[0:D_ATTN]
    vt_ref[0] = kv[D_ATTN:2 * D_ATTN]

    lf = jax.nn.log_sigmoid(_dot_nt(wf_ref[...], xb) + bf_ref[...])
    lf_ref[0] = lf
    carry = carry_ref[...]
    for j in range(rows // LANES):
        cs = _lane_cumsum(lf[:, j * LANES:(j + 1) * LANES], tri_ref[...]) + carry
        c_ref[0, :, j * LANES:(j + 1) * LANES] = cs
        carry = cs[:, LANES - 1:LANES]
    carry_ref[...] = carry

    h = z[:, D_ATTN:D_ATTN + D_CONV]
    gb = z[:, D_ATTN + D_CONV:D_ATTN + 2 * D_CONV]
    gc = z[:, D_ATTN + 2 * D_CONV:D_ATTN + 3 * D_CONV]
    u = gc * h
    hist = hist_ref[...]
    row = lax.broadcasted_iota(jnp.int32, (SUBLANES, D_CONV), 0)
    r1 = pltpu.roll(u, 1, 0)
    r2 = pltpu.roll(u, 2, 0)
    h1 = pltpu.roll(hist, 1, 0)
    h2 = pltpu.roll(hist, 2, 0)
    u1 = jnp.concatenate([jnp.where(row < 1, h1, r1[0:SUBLANES]), r1[SUBLANES:]], axis=0)
    u2 = jnp.concatenate([jnp.where(row < 2, h2, r2[0:SUBLANES]), r2[SUBLANES:]], axis=0)
    cw = cw_ref[...]
    cv = cw[0:1] * u2 + cw[1:2] * u1 + cw[2:3] * u
    cb_ref[...] = (gb * cv).astype(BF16)
    tail = u[rows - SUBLANES:rows]
    hist_ref[...] = tail

    @pl.when(si == pl.num_programs(1) - 1)
    def _():
        st_ref[0] = tail[SUBLANES - (CONV_W - 1):SUBLANES]


def _prompt_proj(x, wn, wkvt, wft, bf, cw, tri, batch, seq):
    rows = PROJ_ROWS
    ns = seq // rows
    row_map = lambda b, s: (b * ns + s, 0)
    const = lambda b, s: (0, 0)
    seq_map = lambda b, s: (b, 0, s)
    return pl.pallas_call(
        _proj_kernel,
        grid=(batch, ns),
        in_specs=[
            pl.BlockSpec((rows, D_MODEL), row_map),
            pl.BlockSpec(wn.shape, const),
            pl.BlockSpec(wkvt.shape, const),
            pl.BlockSpec(wft.shape, const),
            pl.BlockSpec(bf.shape, const),
            pl.BlockSpec(cw.shape, const),
            pl.BlockSpec(tri.shape, const),
        ],
        out_specs=[
            pl.BlockSpec((rows, D_ATTN), row_map),
            pl.BlockSpec((1, D_ATTN, rows), seq_map),
            pl.BlockSpec((1, D_ATTN, rows), seq_map),
            pl.BlockSpec((1, N_HEADS, rows), seq_map),
            pl.BlockSpec((1, N_HEADS, rows), seq_map),
            pl.BlockSpec((rows, D_CONV), row_map),
            pl.BlockSpec((1, CONV_W - 1, D_CONV), lambda b, s: (b, 0, 0)),
        ],
        out_shape=[
            jax.ShapeDtypeStruct((batch * seq, D_ATTN), BF16),
            jax.ShapeDtypeStruct((batch, D_ATTN, seq), F32),
            jax.ShapeDtypeStruct((batch, D_ATTN, seq), F32),
            jax.ShapeDtypeStruct((batch, N_HEADS, seq), F32),
            jax.ShapeDtypeStruct((batch, N_HEADS, seq), F32),
            jax.ShapeDtypeStruct((batch * seq, D_CONV), BF16),
            jax.ShapeDtypeStruct((batch, CONV_W - 1, D_CONV), F32),
        ],
        scratch_shapes=[
            pltpu.VMEM((SUBLANES, D_CONV), F32),
            pltpu.VMEM((N_HEADS, 1), F32),
        ],
        compiler_params=pltpu.CompilerParams(
            dimension_semantics=("arbitrary", "arbitrary"), vmem_limit_bytes=VMEM_LIMIT),
        name="prompt_proj",
    )(x, wn, wkvt, wft, bf, cw, tri)


def _attn_kernel(q_ref, kt_ref, vt_ref, c_ref, o_ref, kb_ref, vb_ref):
    pair = pl.program_id(1)
    qi = pl.program_id(2)
    tq = q_ref.shape[0]
    tk = tq
    seq = kt_ref.shape[2]

    @pl.when(qi == 0)
    def _():
        kt = kt_ref[0]
        vt = vt_ref[0]
        rowi = lax.broadcasted_iota(jnp.int32, (HEAD_DIM, seq), 0)
        ones_row = jnp.where(rowi == 0, 1.0, 0.0).astype(BF16)
        extra = []
        for hh in range(2):
            hi, mid, lo = _split3(-c_ref[0, pl.ds(2 * pair + hh, 1), :])
            e = jnp.where(rowi == 0, hi.astype(F32),
                          jnp.where(rowi == 1, mid.astype(F32),
                                    jnp.where(rowi == 2, lo.astype(F32), 0.0)))
            extra.append(e.astype(BF16))
        kb_ref[0, 0:HEAD_DIM] = kt[0:HEAD_DIM].astype(BF16)
        kb_ref[0, HEAD_DIM:] = extra[0]
        kb_ref[1, 0:HEAD_DIM] = extra[1]
        kb_ref[1, HEAD_DIM:] = kt[HEAD_DIM:].astype(BF16)
        vb_ref[0, 0:HEAD_DIM] = vt[0:HEAD_DIM].astype(BF16)
        vb_ref[0, HEAD_DIM:] = ones_row
        vb_ref[1, 0:HEAD_DIM] = ones_row
        vb_ref[1, HEAD_DIM:] = vt[HEAD_DIM:].astype(BF16)

    q2 = q_ref[...]
    lane = lax.broadcasted_iota(jnp.int32, (1, LANES), 1)
    causal = (lax.broadcasted_iota(jnp.int32, (tq, tk), 1)
              <= lax.broadcasted_iota(jnp.int32, (tq, tk), 0))

    qa = []
    for hh in range(2):
        base = (1 - hh) * HEAD_DIM
        ones3 = jnp.where((lane >= base) & (lane < base + 3), 1.0, 0.0).astype(BF16)
        qa.append(jnp.where((lane // HEAD_DIM) == hh, q2, ones3))

    def step(j, carry, masked):
        col = pl.multiple_of(j * tk, tk)
        if masked:
            half = tq // 2
            chains = [(hh, slice(r, r + half), r + half) for hh in range(2) for r in (0, half)]
        else:
            chains = [(hh, slice(0, tq), tk) for hh in range(2)]

        def logits(hh, rows, ncols):
            return _dot(qa[hh][rows], kb_ref[hh, :, pl.ds(col, ncols)])

        ms, accs = ([], []), ([], [])
        s_next = logits(*chains[0])
        for n, (hh, rows, ncols) in enumerate(chains):
            s = s_next
            if n + 1 < len(chains):
                s_next = logits(*chains[n + 1])
            m, acc = carry[hh]
            if masked:
                s = jnp.where(causal[rows, 0:ncols], s, -jnp.inf)
            m_new = jnp.maximum(m[rows], jnp.max(s, axis=-1, keepdims=True))
            alpha = jnp.exp(m[rows] - m_new)
            p = jnp.exp(s - m_new).astype(BF16)
            pv = _dot_nt(p, vb_ref[hh, :, pl.ds(col, ncols)])
            ms[hh].append(m_new)
            accs[hh].append(alpha * acc[rows] + pv)
        return tuple((jnp.concatenate(ms[hh], axis=0), jnp.concatenate(accs[hh], axis=0))
                     for hh in range(2))

    init = tuple((jnp.full((tq, 1), -jnp.inf, F32), jnp.zeros((tq, LANES), F32)) for _ in range(2))
    carry = lax.fori_loop(0, qi, functools.partial(step, masked=False), init)
    (_, acc0), (_, acc1) = step(qi, carry, True)
    out0 = acc0 / acc0[:, HEAD_DIM:HEAD_DIM + 1]
    out1 = acc1 / acc1[:, 0:1]
    o_ref[...] = jnp.where((lane // HEAD_DIM) == 0, out0, out1).astype(BF16)


def _prompt_attn(q, kt, vt, c, batch, seq):
    t = ATTN_TILE
    nq = seq // t
    pairs = D_ATTN // LANES
    return pl.pallas_call(
        _attn_kernel,
        grid=(batch, pairs, nq),
        in_specs=[
            pl.BlockSpec((t, LANES), lambda b, p, i: (b * nq + i, p)),
            pl.BlockSpec((1, LANES, seq), lambda b, p, i: (b, p, 0)),
            pl.BlockSpec((1, LANES, seq), lambda b, p, i: (b, p, 0)),
            pl.BlockSpec((1, N_HEADS, seq), lambda b, p, i: (b, 0, 0)),
        ],
        out_specs=pl.BlockSpec((t, LANES), lambda b, p, i: (b * nq + i, p)),
        out_shape=jax.ShapeDtypeStruct((batch * seq, D_ATTN), BF16),
        scratch_shapes=[pltpu.VMEM((2, LANES, seq), BF16), pltpu.VMEM((2, LANES, seq), BF16)],
        compiler_params=pltpu.CompilerParams(
            dimension_semantics=("arbitrary", "arbitrary", "arbitrary"),
            vmem_limit_bytes=VMEM_LIMIT),
        name="prompt_attn",
    )(q, kt, vt, c)


def _mix_kernel(alpha, x_ref, a_ref, cb_ref, wg_ref, bg_ref, wpa_ref, wpc_ref, wo_ref,
                g_ref, b_ref, o_ref):
    x = x_ref[...]
    gates = jax.nn.sigmoid(_dot(x.astype(BF16), wg_ref[...]) + bg_ref[...])
    a_out = _dot(a_ref[...], wpa_ref[...])
    c_out = _dot(cb_ref[...], wpc_ref[...])
    m = gates[:, :D_MODEL] * a_out + gates[:, D_MODEL:] * c_out
    tm = _dot(m.astype(BF16), wo_ref[...])
    o_ref[...] = _layer_norm(alpha * x + tm, g_ref[...], b_ref[...])


def _mix(x, a, cb, wg, bg, wpa, wpc, wo, g, b, alpha, rows):
    n = x.shape[0]
    row_map = lambda i: (i, 0)
    const = lambda i: (0, 0)
    return pl.pallas_call(
        functools.partial(_mix_kernel, alpha),
        grid=(n // rows,),
        in_specs=[
            pl.BlockSpec((rows, D_MODEL), row_map),
            pl.BlockSpec((rows, D_ATTN), row_map),
            pl.BlockSpec((rows, D_CONV), row_map),
            pl.BlockSpec(wg.shape, const),
            pl.BlockSpec(bg.shape, const),
            pl.BlockSpec(wpa.shape, const),
            pl.BlockSpec(wpc.shape, const),
            pl.BlockSpec(wo.shape, const),
            pl.BlockSpec(g.shape, const),
            pl.BlockSpec(b.shape, const),
        ],
        out_specs=pl.BlockSpec((rows, D_MODEL), row_map),
        out_shape=jax.ShapeDtypeStruct((n, D_MODEL), F32),
        compiler_params=pltpu.CompilerParams(
            dimension_semantics=("arbitrary",), vmem_limit_bytes=VMEM_LIMIT),
        name="mix",
    )(x, a, cb, wg, bg, wpa, wpc, wo, g, b)


def _ffn_kernel(alpha, x_ref, wg_ref, wu_ref, wd_ref, g_ref, b_ref, o_ref, acc_ref):
    f = pl.program_id(1)
    xb = x_ref[...].astype(BF16)
    act = jax.nn.silu(_dot(xb, wg_ref[...])) * _dot(xb, wu_ref[...])
    part = _dot(act.astype(BF16), wd_ref[...])

    @pl.when(f == 0)
    def _():
        acc_ref[...] = part

    @pl.when(f > 0)
    def _():
        acc_ref[...] += part

    @pl.when(f == pl.num_programs(1) - 1)
    def _():
        o_ref[...] = _layer_norm(alpha * x_ref[...] + acc_ref[...], g_ref[...], b_ref[...])


def _ffn(x, wg, wu, wd, g, b, alpha, rows):
    n = x.shape[0]
    nf = D_FF // FFN_COLS
    return pl.pallas_call(
        functools.partial(_ffn_kernel, alpha),
        grid=(n // rows, nf),
        in_specs=[
            pl.BlockSpec((rows, D_MODEL), lambda i, f: (i, 0)),
            pl.BlockSpec((D_MODEL, FFN_COLS), lambda i, f: (0, f)),
            pl.BlockSpec((D_MODEL, FFN_COLS), lambda i, f: (0, f)),
            pl.BlockSpec((FFN_COLS, D_MODEL), lambda i, f: (f, 0)),
            pl.BlockSpec(g.shape, lambda i, f: (0, 0)),
            pl.BlockSpec(b.shape, lambda i, f: (0, 0)),
        ],
        out_specs=pl.BlockSpec((rows, D_MODEL), lambda i, f: (i, 0)),
        out_shape=jax.ShapeDtypeStruct((n, D_MODEL), F32),
        scratch_shapes=[pltpu.VMEM((rows, D_MODEL), F32)],
        compiler_params=pltpu.CompilerParams(
            dimension_semantics=("arbitrary", "arbitrary"), vmem_limit_bytes=VMEM_LIMIT),
        name="ffn",
    )(x, wg, wu, wd, g, b)


def _sample_proj_kernel(x_ref, wn_ref, wkv_ref, wf_ref, bf_ref, cw_ref, h0_ref, h1_ref,
                        q_ref, k_ref, v_ref, lf_ref, cb_ref, u_ref):
    xb = x_ref[...].astype(BF16)
    z = _dot(xb, wn_ref[...])
    q_ref[...] = z[:, 0:D_ATTN] * SCALE
    kv = _dot_nt(xb, wkv_ref[...])
    k_ref[...] = kv[:, 0:D_ATTN]
    v_ref[...] = kv[:, D_ATTN:2 * D_ATTN]
    lf_ref[...] = jax.nn.log_sigmoid(_dot_nt(wf_ref[...], xb) + bf_ref[...])
    h = z[:, D_ATTN:D_ATTN + D_CONV]
    gb = z[:, D_ATTN + D_CONV:D_ATTN + 2 * D_CONV]
    gc = z[:, D_ATTN + 2 * D_CONV:D_ATTN + 3 * D_CONV]
    u = gc * h
    cw = cw_ref[...]
    cv = cw[0:1] * h0_ref[...] + cw[1:2] * h1_ref[...] + cw[2:3] * u
    cb_ref[...] = (gb * cv).astype(BF16)
    u_ref[...] = u


def _sample_proj(x, wn, wkvt, wft, bf, cw, h0, h1):
    n = x.shape[0]
    args = (x, wn, wkvt, wft, bf, cw, h0, h1)
    full = lambda a: pl.BlockSpec(a.shape, lambda i: (0,) * a.ndim)
    out_shape = [
        jax.ShapeDtypeStruct((n, D_ATTN), F32),
        jax.ShapeDtypeStruct((n, D_ATTN), F32),
        jax.ShapeDtypeStruct((n, D_ATTN), F32),
        jax.ShapeDtypeStruct((N_HEADS, n), F32),
        jax.ShapeDtypeStruct((n, D_CONV), BF16),
        jax.ShapeDtypeStruct((n, D_CONV), F32),
    ]
    return pl.pallas_call(
        _sample_proj_kernel,
        grid=(1,),
        in_specs=[full(a) for a in args],
        out_specs=[full(s) for s in out_shape],
        out_shape=out_shape,
        compiler_params=pltpu.CompilerParams(
            dimension_semantics=("arbitrary",), vmem_limit_bytes=VMEM_LIMIT),
        name="sample_proj",
    )(*args)


def _paged_attn_kernel(pages, pt_ref, q_ref, kn_ref, vn_ref, lfn_ref, tri_ref, *refs):
    k_refs = refs[0:pages]
    v_refs = refs[pages:2 * pages]
    lf_refs = refs[2 * pages:3 * pages]
    o_ref = refs[3 * pages]
    m_ref, l_ref, acc_ref, carry_ref = refs[3 * pages + 1:]
    g = pl.program_id(1)

    @pl.when(g == 0)
    def _():
        m_ref[...] = jnp.full_like(m_ref, -jnp.inf)
        l_ref[...] = jnp.zeros_like(l_ref)
        acc_ref[...] = jnp.zeros_like(acc_ref)
        carry_ref[...] = jnp.zeros_like(carry_ref)

    head = lax.broadcasted_iota(jnp.int32, (N_HEADS, D_ATTN), 0)
    col_head = lax.broadcasted_iota(jnp.int32, (N_HEADS, D_ATTN), 1) // HEAD_DIM
    diag = head == col_head
    qbd = jnp.where(diag, q_ref[0], 0.0).astype(BF16)

    m = m_ref[...]
    l = l_ref[...]
    acc = acc_ref[...]
    carry = carry_ref[...]

    s = jnp.concatenate([_dot(qbd, k_refs[j][0, 0].astype(BF16)) for j in range(pages)], axis=1)
    lf = jnp.concatenate([lf_refs[j][0, 0] for j in range(pages)], axis=0)
    n = pages * N_HEADS
    r = _dot(jnp.concatenate(_split3(lf), axis=0), tri_ref[...])
    r = r[0:n] + r[n:2 * n] + r[2 * n:3 * n]
    cs = r[:, 0:LANES]
    tot = r[:, LANES:2 * LANES]
    c_pages = []
    for j in range(pages):
        c_pages.append(cs[j * N_HEADS:(j + 1) * N_HEADS] + carry)
        carry = carry + tot[j * N_HEADS:(j + 1) * N_HEADS]
    s = s - jnp.concatenate(c_pages, axis=1)
    m_new = jnp.maximum(m, jnp.max(s, axis=-1, keepdims=True))
    alpha = jnp.exp(m - m_new)
    p = jnp.exp(s - m_new)
    l = alpha * l + jnp.sum(p, axis=-1, keepdims=True)
    pb = p.astype(BF16)
    pv = _dot_nt(pb[:, 0:LANES], v_refs[0][0, 0].astype(BF16))
    for j in range(1, pages):
        pv = pv + _dot_nt(pb[:, j * LANES:(j + 1) * LANES], v_refs[j][0, 0].astype(BF16))
    acc = alpha * acc + pv
    m = m_new
    m_ref[...] = m
    l_ref[...] = l
    acc_ref[...] = acc
    carry_ref[...] = carry

    @pl.when(g == pl.num_programs(1) - 1)
    def _():
        kn = kn_ref[0].astype(BF16).astype(F32)
        vn = vn_ref[0].astype(BF16).astype(F32)
        s_new = jnp.sum(qbd.astype(F32) * kn, axis=-1, keepdims=True)
        s_new = s_new - (carry[:, 0:1] + lfn_ref[0])
        m_fin = jnp.maximum(m, s_new)
        alpha = jnp.exp(m - m_fin)
        p_new = jnp.exp(s_new - m_fin)
        l_fin = alpha * l + p_new
        acc_fin = alpha * acc + p_new.astype(BF16).astype(F32) * vn
        out = jnp.where(diag, acc_fin / l_fin, 0.0)
        o_ref[0] = jnp.sum(out, axis=0, keepdims=True)


def _paged_attn(layer, page_table, q, kn, vn, lfn, tri, cache_kt, cache_vt, cache_lft):
    n, n_pages = page_table.shape
    pages = PAGES_PER_STEP
    groups = n_pages // pages
    row = pl.BlockSpec((1, 1, D_ATTN), lambda b, g, pt: (b, 0, 0))

    def page_spec(shape, j):
        return pl.BlockSpec((1, 1) + shape,
                            lambda b, g, pt: (layer, pt[b, g * pages + j], 0, 0))

    in_specs = [row, row, row,
                pl.BlockSpec((1, N_HEADS, 1), lambda b, g, pt: (b, 0, 0)),
                pl.BlockSpec(tri.shape, lambda b, g, pt: (0, 0))]
    in_specs += [page_spec((D_ATTN, PAGE_SIZE), j) for j in range(pages)]
    in_specs += [page_spec((D_ATTN, PAGE_SIZE), j) for j in range(pages)]
    in_specs += [page_spec((N_HEADS, PAGE_SIZE), j) for j in range(pages)]
    grid_spec = pltpu.PrefetchScalarGridSpec(
        num_scalar_prefetch=1,
        grid=(n, groups),
        in_specs=in_specs,
        out_specs=row,
        scratch_shapes=[
            pltpu.VMEM((N_HEADS, 1), F32),
            pltpu.VMEM((N_HEADS, 1), F32),
            pltpu.VMEM((N_HEADS, D_ATTN), F32),
            pltpu.VMEM((N_HEADS, LANES), F32),
        ],
    )
    return pl.pallas_call(
        functools.partial(_paged_attn_kernel, pages),
        grid_spec=grid_spec,
        out_shape=jax.ShapeDtypeStruct((n, 1, D_ATTN), F32),
        compiler_params=pltpu.CompilerParams(
            dimension_semantics=("arbitrary", "arbitrary"), vmem_limit_bytes=VMEM_LIMIT),
        name="paged_attn",
    )(page_table, q, kn, vn, lfn, tri,
      *([cache_kt] * pages), *([cache_vt] * pages), *([cache_lft] * pages))


def kernel(x_prompt, x_sample, cache_k, cache_v, cache_logf, state_conv, page_table, w_in, b_f, b_gate, conv_w, w_attn_proj, w_conv_proj, w_out, ln1_g, ln1_b, w_gate_up, w_down, ln2_g, ln2_b):
    batch, seq, _ = x_prompt.shape
    n_dec = x_sample.shape[0]
    depth = w_in.shape[0]
    n_pool = cache_k.shape[1]
    alpha = (2 * depth) ** 0.25

    cache_kt = jnp.transpose(cache_k, (0, 1, 3, 4, 2)).reshape(depth, n_pool, D_ATTN, PAGE_SIZE)
    cache_vt = jnp.transpose(cache_v, (0, 1, 3, 4, 2)).reshape(depth, n_pool, D_ATTN, PAGE_SIZE)
    cache_lft = jnp.transpose(cache_logf, (0, 1, 3, 2))
    tri = jnp.triu(jnp.ones((LANES, LANES), F32)).astype(BF16)
    tri_ones = jnp.concatenate([tri, jnp.ones((LANES, LANES), BF16)], axis=1)

    xp = x_prompt.reshape(batch * seq, D_MODEL)
    xs = x_sample.reshape(n_dec, D_MODEL)
    kp, vp, lp, cp, ksm, vsm, lsm, csm = [], [], [], [], [], [], [], []
    for l in range(depth):
        wt = jnp.transpose(w_in[l])
        wn = jnp.concatenate([w_in[l][:, OFF_Q:OFF_K], w_in[l][:, OFF_H:OFF_G]], axis=1).astype(BF16)
        wkvt = wt[OFF_K:OFF_F].astype(BF16)
        wft = wt[OFF_F:OFF_H].astype(BF16)
        wg = w_in[l][:, OFF_G:].astype(BF16)
        bf = b_f[l].reshape(N_HEADS, 1)
        bg = b_gate[l].reshape(1, 2 * D_MODEL)
        cw = conv_w[l]
        wpa = w_attn_proj[l].astype(BF16)
        wpc = w_conv_proj[l].astype(BF16)
        wo = w_out[l].astype(BF16)
        wgf = w_gate_up[l][:, :D_FF].astype(BF16)
        wuf = w_gate_up[l][:, D_FF:].astype(BF16)
        wd = w_down[l].astype(BF16)
        g1, b1 = ln1_g[l].reshape(1, D_MODEL), ln1_b[l].reshape(1, D_MODEL)
        g2, b2 = ln2_g[l].reshape(1, D_MODEL), ln2_b[l].reshape(1, D_MODEL)

        q, kt, vt, lf, c, cb, st = _prompt_proj(xp, wn, wkvt, wft, bf, cw, tri, batch, seq)
        a = _prompt_attn(q, kt, vt, c, batch, seq)
        x1 = _mix(xp, a, cb, wg, bg, wpa, wpc, wo, g1, b1, alpha, MIX_ROWS)
        xp = _ffn(x1, wgf, wuf, wd, g2, b2, alpha, FFN_ROWS)
        kp.append(kt); vp.append(vt); lp.append(lf); cp.append(st)

        h0, h1 = state_conv[l, :, 0, :], state_conv[l, :, 1, :]
        qs, ks, vs, lfs, cbs, us = _sample_proj(xs, wn, wkvt, wft, bf, cw, h0, h1)
        a_s = _paged_attn(l, page_table, qs.reshape(n_dec, 1, D_ATTN), ks.reshape(n_dec, 1, D_ATTN),
                          vs.reshape(n_dec, 1, D_ATTN), jnp.transpose(lfs).reshape(n_dec, N_HEADS, 1),
                          tri_ones, cache_kt, cache_vt, cache_lft)
        x1s = _mix(xs, a_s.reshape(n_dec, D_ATTN).astype(BF16), cbs, wg, bg, wpa, wpc, wo, g1, b1,
                   alpha, n_dec)
        xs = _ffn(x1s, wgf, wuf, wd, g2, b2, alpha, n_dec)
        ksm.append(ks); vsm.append(vs); lsm.append(lfs); csm.append(jnp.stack([h1, us], axis=1))

    def heads_last(t):
        return jnp.transpose(t.reshape(depth, batch, N_HEADS, HEAD_DIM, seq), (0, 1, 4, 2, 3))

    y_prompt = xp.reshape(batch, seq, D_MODEL)
    y_sample = xs.reshape(n_dec, 1, D_MODEL)
    k_prompt = heads_last(jnp.stack(kp))
    v_prompt = heads_last(jnp.stack(vp))
    logf_prompt = jnp.transpose(jnp.stack(lp), (0, 1, 3, 2))
    conv_prompt = jnp.stack(cp)
    k_sample = jnp.stack(ksm).reshape(depth, n_dec, 1, N_HEADS, HEAD_DIM)
    v_sample = jnp.stack(vsm).reshape(depth, n_dec, 1, N_HEADS, HEAD_DIM)
    logf_sample = jnp.transpose(jnp.stack(lsm), (0, 2, 1)).reshape(depth, n_dec, 1, N_HEADS)
    conv_sample = jnp.stack(csm)
    return (y_prompt, y_sample, k_prompt, v_prompt, logf_prompt, conv_prompt,
            k_sample, v_sample, logf_sample, conv_sample)
```

```python
import functools

import jax
import jax.numpy as jnp
from jax import lax
from jax.experimental import pallas as pl
from jax.experimental.pallas import tpu as pltpu

D_MODEL = 1024
N_HEADS = 8
HEAD_DIM = 64
D_ATTN = N_HEADS * HEAD_DIM
D_CONV = 512
CONV_W = 3
D_FF = 2816
PAGE_SIZE = 128
LN_EPS = 1e-5
SCALE = HEAD_DIM ** -0.5

OFF_Q = 0
OFF_K = OFF_Q + D_ATTN
OFF_V = OFF_K + D_ATTN
OFF_F = OFF_V + D_ATTN
OFF_H = OFF_F + N_HEADS
OFF_B = OFF_H + D_CONV
OFF_C = OFF_B + D_CONV
OFF_G = OFF_C + D_CONV

LANES = 128
SUBLANES = 8
VMEM_LIMIT = 56 * 1024 * 1024

PROJ_ROWS = 1024
PROJ_COLS = 256
ATTN_TILE = 1024
MIX_ROWS = 1024
MIX_COLS = 256
SUB_ROWS = 256
FFN_COLS = 256

BF16 = jnp.bfloat16
F32 = jnp.float32

_NT = (((1,), (1,)), ((), ()))


def _dot(a, b):
    return jnp.dot(a, b, preferred_element_type=F32)


def _dot_nt(a, b):
    return lax.dot_general(a, b, _NT, preferred_element_type=F32)


def _split3(x):
    hi = x.astype(BF16)
    r1 = x - hi.astype(F32)
    mid = r1.astype(BF16)
    lo = (r1 - mid.astype(F32)).astype(BF16)
    return hi, mid, lo


def _lane_cumsum(x, tri):
    rows = x.shape[0]
    parts = jnp.concatenate(_split3(x), axis=0)
    cs = _dot(parts, tri)
    return cs[:rows] + cs[rows:2 * rows] + cs[2 * rows:]


def _row_subtiles(rows):
    sub = min(rows, SUB_ROWS)
    return [slice(r, r + sub) for r in range(0, rows, sub)]


def _layer_norm(y, g, b):
    mu = jnp.mean(y, axis=-1, keepdims=True)
    d = y - mu
    var = jnp.mean(d * d, axis=-1, keepdims=True)
    return d * lax.rsqrt(var + LN_EPS) * g + b


def _proj_kernel(x_ref, wn_ref, wkv_ref, wf_ref, bf_ref, cw_ref, tri_ref,
                 q_ref, kt_ref, vt_ref, lf_ref, c_ref, cb_ref, st_ref,
                 hist_ref, carry_ref):
    si = pl.program_id(1)
    rows = x_ref.shape[0]

    @pl.when(si == 0)
    def _():
        hist_ref[...] = jnp.zeros_like(hist_ref)
        carry_ref[...] = jnp.zeros_like(carry_ref)

    xb = x_ref[...].astype(BF16)
    pc = PROJ_COLS
    for c in range(0, D_ATTN, pc):
        q_ref[:, c:c + pc] = (_dot(xb, wn_ref[:, c:c + pc]) * SCALE).astype(BF16)
    for r in range(0, D_ATTN, pc):
        kt_ref[0, r:r + pc] = _dot_nt(wkv_ref[r:r + pc], xb)
        vt_ref[0, r:r + pc] = _dot_nt(wkv_ref[D_ATTN + r:D_ATTN + r + pc], xb)

    lf = jax.nn.log_sigmoid(_dot_nt(wf_ref[...], xb) + bf_ref[...])
    lf_ref[0] = lf
    carry = carry_ref[...]
    for j in range(rows // LANES):
        cs = _lane_cumsum(lf[:, j * LANES:(j + 1) * LANES], tri_ref[...]) + carry
        c_ref[0, :, j * LANES:(j + 1) * LANES] = cs
        carry = cs[:, LANES - 1:LANES]
    carry_ref[...] = carry

    row = lax.broadcasted_iota(jnp.int32, (SUBLANES, pc), 0)
    for c in range(0, D_CONV, pc):
        h = _dot(xb, wn_ref[:, D_ATTN + c:D_ATTN + c + pc])
        gb = _dot(xb, wn_ref[:, D_ATTN + D_CONV + c:D_ATTN + D_CONV + c + pc])
        gc = _dot(xb, wn_ref[:, D_ATTN + 2 * D_CONV + c:D_ATTN + 2 * D_CONV + c + pc])
        u = gc * h
        hist = hist_ref[:, c:c + pc]
        r1 = pltpu.roll(u, 1, 0)
        r2 = pltpu.roll(u, 2, 0)
        h1 = pltpu.roll(hist, 1, 0)
        h2 = pltpu.roll(hist, 2, 0)
        u1 = jnp.concatenate([jnp.where(row < 1, h1, r1[0:SUBLANES]), r1[SUBLANES:]], axis=0)
        u2 = jnp.concatenate([jnp.where(row < 2, h2, r2[0:SUBLANES]), r2[SUBLANES:]], axis=0)
        cw = cw_ref[:, c:c + pc]
        cv = cw[0:1] * u2 + cw[1:2] * u1 + cw[2:3] * u
        cb_ref[:, c:c + pc] = (gb * cv).astype(BF16)
        hist_ref[:, c:c + pc] = u[rows - SUBLANES:rows]

    @pl.when(si == pl.num_programs(1) - 1)
    def _():
        st_ref[0] = hist_ref[SUBLANES - (CONV_W - 1):SUBLANES, :]


def _prompt_proj(x, wn, wkvt, wft, bf, cw, tri, batch, seq):
    rows = PROJ_ROWS
    ns = seq // rows
    row_map = lambda b, s: (b * ns + s, 0)
    const = lambda b, s: (0, 0)
    seq_map = lambda b, s: (b, 0, s)
    return pl.pallas_call(
        _proj_kernel,
        grid=(batch, ns),
        in_specs=[
            pl.BlockSpec((rows, D_MODEL), row_map),
            pl.BlockSpec(wn.shape, const),
            pl.BlockSpec(wkvt.shape, const),
            pl.BlockSpec(wft.shape, const),
            pl.BlockSpec(bf.shape, const),
            pl.BlockSpec(cw.shape, const),
            pl.BlockSpec(tri.shape, const),
        ],
        out_specs=[
            pl.BlockSpec((rows, D_ATTN), row_map),
            pl.BlockSpec((1, D_ATTN, rows), seq_map),
            pl.BlockSpec((1, D_ATTN, rows), seq_map),
            pl.BlockSpec((1, N_HEADS, rows), seq_map),
            pl.BlockSpec((1, N_HEADS, rows), seq_map),
            pl.BlockSpec((rows, D_CONV), row_map),
            pl.BlockSpec((1, CONV_W - 1, D_CONV), lambda b, s: (b, 0, 0)),
        ],
        out_shape=[
            jax.ShapeDtypeStruct((batch * seq, D_ATTN), BF16),
            jax.ShapeDtypeStruct((batch, D_ATTN, seq), F32),
            jax.ShapeDtypeStruct((batch, D_ATTN, seq), F32),
            jax.ShapeDtypeStruct((batch, N_HEADS, seq), F32),
            jax.ShapeDtypeStruct((batch, N_HEADS, seq), F32),
            jax.ShapeDtypeStruct((batch * seq, D_CONV), BF16),
            jax.ShapeDtypeStruct((batch, CONV_W - 1, D_CONV), F32),
        ],
        scratch_shapes=[
            pltpu.VMEM((SUBLANES, D_CONV), F32),
            pltpu.VMEM((N_HEADS, 1), F32),
        ],
        compiler_params=pltpu.CompilerParams(
            dimension_semantics=("arbitrary", "arbitrary"), vmem_limit_bytes=VMEM_LIMIT),
        name="prompt_proj",
    )(x, wn, wkvt, wft, bf, cw, tri)


def _attn_kernel(q_ref, kt_ref, vt_ref, c_ref, o_ref, kb_ref, vb_ref):
    pair = pl.program_id(1)
    qi = pl.program_id(2)
    tq = q_ref.shape[0]
    tk = tq
    seq = kt_ref.shape[2]

    @pl.when(qi == 0)
    def _():
        kt = kt_ref[0]
        vt = vt_ref[0]
        rowi = lax.broadcasted_iota(jnp.int32, (HEAD_DIM, seq), 0)
        ones_row = jnp.where(rowi == 0, 1.0, 0.0).astype(BF16)
        extra = []
        for hh in range(2):
            hi, mid, lo = _split3(-c_ref[0, pl.ds(2 * pair + hh, 1), :])
            e = jnp.where(rowi == 0, hi.astype(F32),
                          jnp.where(rowi == 1, mid.astype(F32),
                                    jnp.where(rowi == 2, lo.astype(F32), 0.0)))
            extra.append(e.astype(BF16))
        kb_ref[0, 0:HEAD_DIM] = kt[0:HEAD_DIM].astype(BF16)
        kb_ref[0, HEAD_DIM:] = extra[0]
        kb_ref[1, 0:HEAD_DIM] = extra[1]
        kb_ref[1, HEAD_DIM:] = kt[HEAD_DIM:].astype(BF16)
        vb_ref[0, 0:HEAD_DIM] = vt[0:HEAD_DIM].astype(BF16)
        vb_ref[0, HEAD_DIM:] = ones_row
        vb_ref[1, 0:HEAD_DIM] = ones_row
        vb_ref[1, HEAD_DIM:] = vt[HEAD_DIM:].astype(BF16)

    q2 = q_ref[...]
    lane = lax.broadcasted_iota(jnp.int32, (1, LANES), 1)
    causal = (lax.broadcasted_iota(jnp.int32, (tq, tk), 1)
              <= lax.broadcasted_iota(jnp.int32, (tq, tk), 0))

    qa = []
    for hh in range(2):
        base = (1 - hh) * HEAD_DIM
        ones3 = jnp.where((lane >= base) & (lane < base + 3), 1.0, 0.0).astype(BF16)
        qa.append(jnp.where((lane // HEAD_DIM) == hh, q2, ones3))

    def step(j, carry, masked):
        col = pl.multiple_of(j * tk, tk)
        if masked:
            half = tq // 2
            chains = [(hh, slice(r, r + half), r + half) for hh in range(2) for r in (0, half)]
        else:
            chains = [(hh, slice(0, tq), tk) for hh in range(2)]

        def logits(hh, rows, ncols):
            return _dot(qa[hh][rows], kb_ref[hh, :, pl.ds(col, ncols)])

        ms, accs = ([], []), ([], [])
        s_next = logits(*chains[0])
        for n, (hh, rows, ncols) in enumerate(chains):
            s = s_next
            if n + 1 < len(chains):
                s_next = logits(*chains[n + 1])
            m, acc = carry[hh]
            if masked:
                s = jnp.where(causal[rows, 0:ncols], s, -jnp.inf)
            m_new = jnp.maximum(m[rows], jnp.max(s, axis=-1, keepdims=True))
            alpha = jnp.exp(m[rows] - m_new)
            p = jnp.exp(s - m_new).astype(BF16)
            pv = _dot_nt(p, vb_ref[hh, :, pl.ds(col, ncols)])
            ms[hh].append(m_new)
            accs[hh].append(alpha * acc[rows] + pv)
        return tuple((jnp.concatenate(ms[hh], axis=0), jnp.concatenate(accs[hh], axis=0))
                     for hh in range(2))

    init = tuple((jnp.full((tq, 1), -jnp.inf, F32), jnp.zeros((tq, LANES), F32)) for _ in range(2))
    carry = lax.fori_loop(0, qi, functools.partial(step, masked=False), init)
    (_, acc0), (_, acc1) = step(qi, carry, True)
    out0 = acc0 / acc0[:, HEAD_DIM:HEAD_DIM + 1]
    out1 = acc1 / acc1[:, 0:1]
    o_ref[...] = jnp.where((lane // HEAD_DIM) == 0, out0, out1).astype(BF16)


def _prompt_attn(q, kt, vt, c, batch, seq):
    t = ATTN_TILE
    nq = seq // t
    pairs = D_ATTN // LANES
    return pl.pallas_call(
        _attn_kernel,
        grid=(batch, pairs, nq),
        in_specs=[
            pl.BlockSpec((t, LANES), lambda b, p, i: (b * nq + i, p)),
            pl.BlockSpec((1, LANES, seq), lambda b, p, i: (b, p, 0)),
            pl.BlockSpec((1, LANES, seq), lambda b, p, i: (b, p, 0)),
            pl.BlockSpec((1, N_HEADS, seq), lambda b, p, i: (b, 0, 0)),
        ],
        out_specs=pl.BlockSpec((t, LANES), lambda b, p, i: (b * nq + i, p)),
        out_shape=jax.ShapeDtypeStruct((batch * seq, D_ATTN), BF16),
        scratch_shapes=[pltpu.VMEM((2, LANES, seq), BF16), pltpu.VMEM((2, LANES, seq), BF16)],
        compiler_params=pltpu.CompilerParams(
            dimension_semantics=("arbitrary", "arbitrary", "arbitrary"),
            vmem_limit_bytes=VMEM_LIMIT),
        name="prompt_attn",
    )(q, kt, vt, c)


def _mix_kernel(alpha, x_ref, a_ref, cb_ref, wg_ref, bg_ref, wpa_ref, wpc_ref, wo_ref,
                g_ref, b_ref, o_ref, m_ref):
    subs = _row_subtiles(x_ref.shape[0])
    for rows in subs:
        xb = x_ref[rows].astype(BF16)
        a = a_ref[rows]
        cb = cb_ref[rows]
        for c in range(0, D_MODEL, MIX_COLS):
            cols = slice(c, c + MIX_COLS)
            gcols = slice(D_MODEL + c, D_MODEL + c + MIX_COLS)
            ga = jax.nn.sigmoid(_dot(xb, wg_ref[:, cols]) + bg_ref[:, cols])
            gc = jax.nn.sigmoid(_dot(xb, wg_ref[:, gcols]) + bg_ref[:, gcols])
            m = ga * _dot(a, wpa_ref[:, cols]) + gc * _dot(cb, wpc_ref[:, cols])
            m_ref[rows, cols] = m.astype(BF16)
    for rows in subs:
        tm = _dot(m_ref[rows], wo_ref[...])
        o_ref[rows] = _layer_norm(alpha * x_ref[rows] + tm, g_ref[...], b_ref[...])


def _mix(x, a, cb, wg, bg, wpa, wpc, wo, g, b, alpha, rows):
    n = x.shape[0]
    row_map = lambda i: (i, 0)
    const = lambda i: (0, 0)
    return pl.pallas_call(
        functools.partial(_mix_kernel, alpha),
        grid=(n // rows,),
        in_specs=[
            pl.BlockSpec((rows, D_MODEL), row_map),
            pl.BlockSpec((rows, D_ATTN), row_map),
            pl.BlockSpec((rows, D_CONV), row_map),
            pl.BlockSpec(wg.shape, const),
            pl.BlockSpec(bg.shape, const),
            pl.BlockSpec(wpa.shape, const),
            pl.BlockSpec(wpc.shape, const),
            pl.BlockSpec(wo.shape, const),
            pl.BlockSpec(g.shape, const),
            pl.BlockSpec(b.shape, const),
        ],
        out_specs=pl.BlockSpec((rows, D_MODEL), row_map),
        out_shape=jax.ShapeDtypeStruct((n, D_MODEL), F32),
        scratch_shapes=[pltpu.VMEM((rows, D_MODEL), BF16)],
        compiler_params=pltpu.CompilerParams(
            dimension_semantics=("arbitrary",), vmem_limit_bytes=VMEM_LIMIT),
        name="mix",
    )(x, a, cb, wg, bg, wpa, wpc, wo, g, b)


def _ffn_body(alpha, x_ref, wgu_ref, wd_ref, g_ref, b_ref, o_ref, act_ref):
    subs = _row_subtiles(x_ref.shape[0])
    for rows in subs:
        xb = x_ref[rows].astype(BF16)
        for c in range(0, D_FF, FFN_COLS):
            gate = _dot(xb, wgu_ref[:, c:c + FFN_COLS])
            up = _dot(xb, wgu_ref[:, D_FF + c:D_FF + c + FFN_COLS])
            act_ref[rows, c:c + FFN_COLS] = (jax.nn.silu(gate) * up).astype(BF16)
    for rows in subs:
        f = _dot(act_ref[rows], wd_ref[...])
        o_ref[rows] = _layer_norm(alpha * x_ref[rows] + f, g_ref[...], b_ref[...])


def _resident(a):
    return pl.BlockSpec(a.shape, lambda *_: (0,) * a.ndim, pipeline_mode=pl.Buffered(1))


def _ffn(x, wgu, wd, g, b, alpha, rows):
    n = x.shape[0]
    return pl.pallas_call(
        functools.partial(_ffn_body, alpha),
        grid=(n // rows,),
        in_specs=[
            pl.BlockSpec((rows, D_MODEL), lambda i: (i, 0)),
            _resident(wgu), _resident(wd), _resident(g), _resident(b),
        ],
        out_specs=pl.BlockSpec((rows, D_MODEL), lambda i: (i, 0)),
        out_shape=jax.ShapeDtypeStruct((n, D_MODEL), F32),
        scratch_shapes=[pltpu.VMEM((rows, D_FF), BF16)],
        compiler_params=pltpu.CompilerParams(
            dimension_semantics=("arbitrary",), vmem_limit_bytes=VMEM_LIMIT),
        name="ffn",
    )(x, wgu, wd, g, b)


def _sample_proj_kernel(x_ref, wn_ref, wkv_ref, wf_ref, bf_ref, cw_ref, h0_ref, h1_ref,
                        q_ref, k_ref, v_ref, lf_ref, cb_ref, u_ref):
    xb = x_ref[...].astype(BF16)
    z = _dot(xb, wn_ref[...])
    q_ref[...] = z[:, 0:D_ATTN] * SCALE
    kv = _dot_nt(xb, wkv_ref[...])
    k_ref[...] = kv[:, 0:D_ATTN]
    v_ref[...] = kv[:, D_ATTN:2 * D_ATTN]
    lf_ref[...] = jax.nn.log_sigmoid(_dot_nt(wf_ref[...], xb) + bf_ref[...])
    h = z[:, D_ATTN:D_ATTN + D_CONV]
    gb = z[:, D_ATTN + D_CONV:D_ATTN + 2 * D_CONV]
    gc = z[:, D_ATTN + 2 * D_CONV:D_ATTN + 3 * D_CONV]
    u = gc * h
    cw = cw_ref[...]
    cv = cw[0:1] * h0_ref[...] + cw[1:2] * h1_ref[...] + cw[2:3] * u
    cb_ref[...] = (gb * cv).astype(BF16)
    u_ref[...] = u


def _sample_proj(x, wn, wkvt, wft, bf, cw, h0, h1):
    n = x.shape[0]
    args = (x, wn, wkvt, wft, bf, cw, h0, h1)
    full = lambda a: pl.BlockSpec(a.shape, lambda i: (0,) * a.ndim)
    out_shape = [
        jax.ShapeDtypeStruct((n, D_ATTN), F32),
        jax.ShapeDtypeStruct((n, D_ATTN), F32),
        jax.ShapeDtypeStruct((n, D_ATTN), F32),
        jax.ShapeDtypeStruct((N_HEADS, n), F32),
        jax.ShapeDtypeStruct((n, D_CONV), BF16),
        jax.ShapeDtypeStruct((n, D_CONV), F32),
    ]
    return pl.pallas_call(
        _sample_proj_kernel,
        grid=(1,),
        in_specs=[full(a) for a in args],
        out_specs=[full(s) for s in out_shape],
        out_shape=out_shape,
        compiler_params=pltpu.CompilerParams(
            dimension_semantics=("arbitrary",), vmem_limit_bytes=VMEM_LIMIT),
        name="sample_proj",
    )(*args)


def _paged_logits(qbd, k_refs):
    return jnp.concatenate([_dot(qbd, k[0, 0].astype(BF16)) for k in k_refs], axis=1)


def _paged_update(s, lf_refs, v_refs, tri_ref, m_ref, l_ref, acc_ref, carry_ref):
    pages = len(v_refs)
    m = m_ref[...]
    carry = carry_ref[...]
    lf = jnp.concatenate([r[0, 0] for r in lf_refs], axis=0)
    n = pages * N_HEADS
    r = _dot(jnp.concatenate(_split3(lf), axis=0), tri_ref[...])
    r = r[0:n] + r[n:2 * n] + r[2 * n:3 * n]
    cs = r[:, 0:LANES]
    tot = r[:, LANES:2 * LANES]
    c_pages = []
    for j in range(pages):
        c_pages.append(cs[j * N_HEADS:(j + 1) * N_HEADS] + carry)
        carry = carry + tot[j * N_HEADS:(j + 1) * N_HEADS]
    s = s - jnp.concatenate(c_pages, axis=1)
    m_new = jnp.maximum(m, jnp.max(s, axis=-1, keepdims=True))
    alpha = jnp.exp(m - m_new)
    p = jnp.exp(s - m_new)
    pb = p.astype(BF16)
    pv = _dot_nt(pb[:, 0:LANES], v_refs[0][0, 0].astype(BF16))
    for j in range(1, pages):
        pv = pv + _dot_nt(pb[:, j * LANES:(j + 1) * LANES], v_refs[j][0, 0].astype(BF16))
    m_ref[...] = m_new
    l_ref[...] = alpha * l_ref[...] + jnp.sum(p, axis=-1, keepdims=True)
    acc_ref[...] = alpha * acc_ref[...] + pv
    carry_ref[...] = carry


def _paged_finish(qbd, diag, kn_ref, vn_ref, lfn_ref, m_ref, l_ref, acc_ref, carry_ref, o_ref):
    m = m_ref[...]
    kn = kn_ref[0].astype(BF16).astype(F32)
    vn = vn_ref[0].astype(BF16).astype(F32)
    s_new = jnp.sum(qbd.astype(F32) * kn, axis=-1, keepdims=True)
    s_new = s_new - (carry_ref[:, 0:1] + lfn_ref[0])
    m_fin = jnp.maximum(m, s_new)
    alpha = jnp.exp(m - m_fin)
    p_new = jnp.exp(s_new - m_fin)
    l_fin = alpha * l_ref[...] + p_new
    acc_fin = alpha * acc_ref[...] + p_new.astype(BF16).astype(F32) * vn
    out = jnp.where(diag, acc_fin / l_fin, 0.0)
    o_ref[0] = jnp.sum(out, axis=0, keepdims=True)


def _ffn_paged_kernel(alpha, pages, pt_ref, x_ref, wgu_ref, wd_ref, g_ref, b_ref,
                      q_ref, kn_ref, vn_ref, lfn_ref, tri_ref, *refs):
    k_refs = refs[0:pages]
    v_refs = refs[pages:2 * pages]
    lf_refs = refs[2 * pages:3 * pages]
    y_ref, o_ref = refs[3 * pages:3 * pages + 2]
    act_ref, part_ref, m_ref, l_ref, acc_ref, carry_ref = refs[3 * pages + 2:]
    state = (m_ref, l_ref, acc_ref, carry_ref)
    quarter = pl.program_id(1)

    head = lax.broadcasted_iota(jnp.int32, (N_HEADS, D_ATTN), 0)
    col_head = lax.broadcasted_iota(jnp.int32, (N_HEADS, D_ATTN), 1) // HEAD_DIM
    diag = head == col_head
    qbd = jnp.where(diag, q_ref[0], 0.0).astype(BF16)
    subs = _row_subtiles(x_ref.shape[0])

    def gate_up(c0, c1):
        for rows in subs:
            xb = x_ref[rows].astype(BF16)
            for c in range(c0, c1, FFN_COLS):
                gate = _dot(xb, wgu_ref[:, c:c + FFN_COLS])
                up = _dot(xb, wgu_ref[:, D_FF + c:D_FF + c + FFN_COLS])
                act_ref[rows, c:c + FFN_COLS] = (jax.nn.silu(gate) * up).astype(BF16)

    def down(k0, k1, first=False, last=False):
        for rows in subs:
            part = _dot(act_ref[rows, k0:k1], wd_ref[k0:k1, :])
            if first:
                part_ref[rows] = part
            elif not last:
                part_ref[rows] += part
            else:
                y_ref[rows] = _layer_norm(alpha * x_ref[rows] + (part_ref[rows] + part),
                                          g_ref[...], b_ref[...])

    c1, c2, c3, c4 = 2 * FFN_COLS, 4 * FFN_COLS, 6 * FFN_COLS, 8 * FFN_COLS
    work = [
        (lambda: gate_up(0, c1), lambda: gate_up(c1, c2)),
        (lambda: gate_up(c2, c3), lambda: gate_up(c3, c4)),
        (lambda: gate_up(c4, D_FF), lambda: down(0, c2, first=True)),
        (lambda: down(c2, c4), lambda: down(c4, D_FF, last=True)),
    ]
    for k, (before, after) in enumerate(work):
        @pl.when(quarter == k)
        def _(k=k, before=before, after=after):
            if k == 0:
                m_ref[...] = jnp.full_like(m_ref, -jnp.inf)
                l_ref[...] = jnp.zeros_like(l_ref)
                acc_ref[...] = jnp.zeros_like(acc_ref)
                carry_ref[...] = jnp.zeros_like(carry_ref)
            s = _paged_logits(qbd, k_refs)
            before()
            _paged_update(s, lf_refs, v_refs, tri_ref, *state)
            after()
            if k == len(work) - 1:
                _paged_finish(qbd, diag, kn_ref, vn_ref, lfn_ref, *state, o_ref)


def _ffn_paged(layer, page_table, x, wgu, wd, g, b, alpha, q, kn, vn, lfn, tri,
               cache_kt, cache_vt, cache_lft):
    n, n_pages = page_table.shape
    quarters = 4
    assert n_pages % quarters == 0 and x.shape[0] % n == 0
    pages = n_pages // quarters
    rows = x.shape[0] // n
    assert rows % SUBLANES == 0
    x_spec = pl.BlockSpec((rows, D_MODEL), lambda s, k, pt: (s, 0))
    row = pl.BlockSpec((1, 1, D_ATTN), lambda s, k, pt: (s, 0, 0))

    def page_spec(shape, j):
        return pl.BlockSpec((1, 1) + shape,
                            lambda s, k, pt: (layer, pt[s, k * pages + j], 0, 0))

    in_specs = [x_spec, _resident(wgu), _resident(wd), _resident(g), _resident(b),
                row, row, row,
                pl.BlockSpec((1, N_HEADS, 1), lambda s, k, pt: (s, 0, 0)),
                _resident(tri)]
    in_specs += [page_spec((D_ATTN, PAGE_SIZE), j) for j in range(pages)]
    in_specs += [page_spec((D_ATTN, PAGE_SIZE), j) for j in range(pages)]
    in_specs += [page_spec((N_HEADS, PAGE_SIZE), j) for j in range(pages)]
    grid_spec = pltpu.PrefetchScalarGridSpec(
        num_scalar_prefetch=1,
        grid=(n, quarters),
        in_specs=in_specs,
        out_specs=[x_spec, row],
        scratch_shapes=[
            pltpu.VMEM((rows, D_FF), BF16),
            pltpu.VMEM((rows, D_MODEL), F32),
            pltpu.VMEM((N_HEADS, 1), F32),
            pltpu.VMEM((N_HEADS, 1), F32),
            pltpu.VMEM((N_HEADS, D_ATTN), F32),
            pltpu.VMEM((N_HEADS, LANES), F32),
        ],
    )
    return pl.pallas_call(
        functools.partial(_ffn_paged_kernel, alpha, pages),
        grid_spec=grid_spec,
        out_shape=[jax.ShapeDtypeStruct(x.shape, F32),
                   jax.ShapeDtypeStruct((n, 1, D_ATTN), F32)],
        compiler_params=pltpu.CompilerParams(
            dimension_semantics=("arbitrary", "arbitrary"), vmem_limit_bytes=VMEM_LIMIT),
        name="ffn_paged",
    )(page_table, x, wgu, wd, g, b, q, kn, vn, lfn, tri,
      *([cache_kt] * pages), *([cache_vt] * pages), *([cache_lft] * pages))


def kernel(x_prompt, x_sample, cache_k, cache_v, cache_logf, state_conv, page_table, w_in, b_f, b_gate, conv_w, w_attn_proj, w_conv_proj, w_out, ln1_g, ln1_b, w_gate_up, w_down, ln2_g, ln2_b):
    batch, seq, _ = x_prompt.shape
    n_dec = x_sample.shape[0]
    depth = w_in.shape[0]
    n_pool = cache_k.shape[1]
    alpha = (2 * depth) ** 0.25

    cache_kt = jnp.transpose(cache_k, (0, 1, 3, 4, 2)).reshape(depth, n_pool, D_ATTN, PAGE_SIZE)
    cache_vt = jnp.transpose(cache_v, (0, 1, 3, 4, 2)).reshape(depth, n_pool, D_ATTN, PAGE_SIZE)
    cache_lft = jnp.transpose(cache_logf, (0, 1, 3, 2))
    tri = jnp.triu(jnp.ones((LANES, LANES), F32)).astype(BF16)
    tri_ones = jnp.concatenate([tri, jnp.ones((LANES, LANES), BF16)], axis=1)

    xp = x_prompt.reshape(batch * seq, D_MODEL)
    xs = x_sample.reshape(n_dec, D_MODEL)
    kp, vp, lp, cp, ksm, vsm, lsm, csm = [], [], [], [], [], [], [], []
    for l in range(depth):
        wt = jnp.transpose(w_in[l])
        wn = jnp.concatenate([w_in[l][:, OFF_Q:OFF_K], w_in[l][:, OFF_H:OFF_G]], axis=1).astype(BF16)
        wkvt = wt[OFF_K:OFF_F].astype(BF16)
        wft = wt[OFF_F:OFF_H].astype(BF16)
        wg = w_in[l][:, OFF_G:].astype(BF16)
        bf = b_f[l].reshape(N_HEADS, 1)
        bg = b_gate[l].reshape(1, 2 * D_MODEL)
        cw = conv_w[l]
        wpa = w_attn_proj[l].astype(BF16)
        wpc = w_conv_proj[l].astype(BF16)
        wo = w_out[l].astype(BF16)
        wgu = w_gate_up[l].astype(BF16)
        wd = w_down[l].astype(BF16)
        g1, b1 = ln1_g[l].reshape(1, D_MODEL), ln1_b[l].reshape(1, D_MODEL)
        g2, b2 = ln2_g[l].reshape(1, D_MODEL), ln2_b[l].reshape(1, D_MODEL)

        q, kt, vt, lf, c, cb, st = _prompt_proj(xp, wn, wkvt, wft, bf, cw, tri, batch, seq)
        a = _prompt_attn(q, kt, vt, c, batch, seq)
        x1 = _mix(xp, a, cb, wg, bg, wpa, wpc, wo, g1, b1, alpha, MIX_ROWS)
        kp.append(kt); vp.append(vt); lp.append(lf); cp.append(st)
        h0, h1 = state_conv[l, :, 0, :], state_conv[l, :, 1, :]
        qs, ks, vs, lfs, cbs, us = _sample_proj(xs, wn, wkvt, wft, bf, cw, h0, h1)

        xp, a_s = _ffn_paged(l, page_table, x1, wgu, wd, g2, b2, alpha,
                             qs.reshape(n_dec, 1, D_ATTN), ks.reshape(n_dec, 1, D_ATTN),
                             vs.reshape(n_dec, 1, D_ATTN),
                             jnp.transpose(lfs).reshape(n_dec, N_HEADS, 1),
                             tri_ones, cache_kt, cache_vt, cache_lft)

        x1s = _mix(xs, a_s.reshape(n_dec, D_ATTN).astype(BF16), cbs, wg, bg, wpa, wpc, wo, g1, b1,
                   alpha, n_dec)
        xs = _ffn(x1s, wgu, wd, g2, b2, alpha, n_dec)
        ksm.append(ks); vsm.append(vs); lsm.append(lfs); csm.append(jnp.stack([h1, us], axis=1))

    def heads_last(t):
        return jnp.transpose(t.reshape(depth, batch, N_HEADS, HEAD_DIM, seq), (0, 1, 4, 2, 3))

    y_prompt = xp.reshape(batch, seq, D_MODEL)
    y_sample = xs.reshape(n_dec, 1, D_MODEL)
    k_prompt = heads_last(jnp.stack(kp))
    v_prompt = heads_last(jnp.stack(vp))
    logf_prompt = jnp.transpose(jnp.stack(lp), (0, 1, 3, 2))
    conv_prompt = jnp.stack(cp)
    k_sample = jnp.stack(ksm).reshape(depth, n_dec, 1, N_HEADS, HEAD_DIM)
    v_sample = jnp.stack(vsm).reshape(depth, n_dec, 1, N_HEADS, HEAD_DIM)
    logf_sample = jnp.transpose(jnp.stack(lsm), (0, 2, 1)).reshape(depth, n_dec, 1, N_HEADS)
    conv_sample = jnp.stack(csm)
    return (y_prompt, y_sample, k_prompt, v_prompt, logf_prompt, conv_prompt,
            k_sample, v_sample, logf_sample, conv_sample)
```

```python
import functools

import jax
import jax.numpy as jnp
from jax import lax
from jax.experimental import pallas as pl
from jax.experimental.pallas import tpu as pltpu

D_MODEL = 1024
N_HEADS = 8
HEAD_DIM = 64
D_ATTN = N_HEADS * HEAD_DIM
D_CONV = 512
CONV_W = 3
D_FF = 2816
PAGE_SIZE = 128
LN_EPS = 1e-5
SCALE = HEAD_DIM ** -0.5

OFF_Q = 0
OFF_K = OFF_Q + D_ATTN
OFF_V = OFF_K + D_ATTN
OFF_F = OFF_V + D_ATTN
OFF_H = OFF_F + N_HEADS
OFF_B = OFF_H + D_CONV
OFF_C = OFF_B + D_CONV
OFF_G = OFF_C + D_CONV

LANES = 128
SUBLANES = 8
VMEM_LIMIT = 56 * 1024 * 1024

PROJ_ROWS = 1024
PROJ_COLS = 256
ATTN_TILE = 1024
MIX_ROWS = 1024
MIX_COLS = 256
SUB_ROWS = 256
FFN_COLS = 256

BF16 = jnp.bfloat16
F32 = jnp.float32

_NT = (((1,), (1,)), ((), ()))


def _dot(a, b):
    return jnp.dot(a, b, preferred_element_type=F32)


def _dot_nt(a, b):
    return lax.dot_general(a, b, _NT, preferred_element_type=F32)


def _split3(x):
    hi = x.astype(BF16)
    r1 = x - hi.astype(F32)
    mid = r1.astype(BF16)
    lo = (r1 - mid.astype(F32)).astype(BF16)
    return hi, mid, lo


def _lane_cumsum(x, tri):
    rows = x.shape[0]
    parts = jnp.concatenate(_split3(x), axis=0)
    cs = _dot(parts, tri)
    return cs[:rows] + cs[rows:2 * rows] + cs[2 * rows:]


def _row_subtiles(rows):
    sub = min(rows, SUB_ROWS)
    return [slice(r, r + sub) for r in range(0, rows, sub)]


def _layer_norm(y, g, b):
    mu = jnp.mean(y, axis=-1, keepdims=True)
    d = y - mu
    var = jnp.mean(d * d, axis=-1, keepdims=True)
    return d * lax.rsqrt(var + LN_EPS) * g + b


def _proj_kernel(x_ref, wn_ref, wkv_ref, wf_ref, bf_ref, cw_ref, tri_ref,
                 q_ref, kt_ref, vt_ref, lf_ref, c_ref, cb_ref, st_ref,
                 hist_ref, carry_ref):
    si = pl.program_id(1)
    rows = x_ref.shape[0]

    @pl.when(si == 0)
    def _():
        hist_ref[...] = jnp.zeros_like(hist_ref)
        carry_ref[...] = jnp.zeros_like(carry_ref)

    xb = x_ref[...].astype(BF16)
    pc = PROJ_COLS
    for c in range(0, D_ATTN, pc):
        q_ref[:, c:c + pc] = (_dot(xb, wn_ref[:, c:c + pc]) * SCALE).astype(BF16)
    for r in range(0, D_ATTN, pc):
        kt_ref[0, r:r + pc] = _dot_nt(wkv_ref[r:r + pc], xb)
        vt_ref[0, r:r + pc] = _dot_nt(wkv_ref[D_ATTN + r:D_ATTN + r + pc], xb)

    lf = jax.nn.log_sigmoid(_dot_nt(wf_ref[...], xb) + bf_ref[...])
    lf_ref[0] = lf
    carry = carry_ref[...]
    for j in range(rows // LANES):
        cs = _lane_cumsum(lf[:, j * LANES:(j + 1) * LANES], tri_ref[...]) + carry
        c_ref[0, :, j * LANES:(j + 1) * LANES] = cs
        carry = cs[:, LANES - 1:LANES]
    carry_ref[...] = carry

    row = lax.broadcasted_iota(jnp.int32, (SUBLANES, pc), 0)
    for c in range(0, D_CONV, pc):
        h = _dot(xb, wn_ref[:, D_ATTN + c:D_ATTN + c + pc])
        gb = _dot(xb, wn_ref[:, D_ATTN + D_CONV + c:D_ATTN + D_CONV + c + pc])
        gc = _dot(xb, wn_ref[:, D_ATTN + 2 * D_CONV + c:D_ATTN + 2 * D_CONV + c + pc])
        u = gc * h
        hist = hist_ref[:, c:c + pc]
        r1 = pltpu.roll(u, 1, 0)
        r2 = pltpu.roll(u, 2, 0)
        h1 = pltpu.roll(hist, 1, 0)
        h2 = pltpu.roll(hist, 2, 0)
        u1 = jnp.concatenate([jnp.where(row < 1, h1, r1[0:SUBLANES]), r1[SUBLANES:]], axis=0)
        u2 = jnp.concatenate([jnp.where(row < 2, h2, r2[0:SUBLANES]), r2[SUBLANES:]], axis=0)
        cw = cw_ref[:, c:c + pc]
        cv = cw[0:1] * u2 + cw[1:2] * u1 + cw[2:3] * u
        cb_ref[:, c:c + pc] = (gb * cv).astype(BF16)
        hist_ref[:, c:c + pc] = u[rows - SUBLANES:rows]

    @pl.when(si == pl.num_programs(1) - 1)
    def _():
        st_ref[0] = hist_ref[SUBLANES - (CONV_W - 1):SUBLANES, :]


def _prompt_proj(x, wn, wkvt, wft, bf, cw, tri, batch, seq):
    rows = PROJ_ROWS
    ns = seq // rows
    row_map = lambda b, s: (b * ns + s, 0)
    const = lambda b, s: (0, 0)
    seq_map = lambda b, s: (b, 0, s)
    return pl.pallas_call(
        _proj_kernel,
        grid=(batch, ns),
        in_specs=[
            pl.BlockSpec((rows, D_MODEL), row_map),
            pl.BlockSpec(wn.shape, const),
            pl.BlockSpec(wkvt.shape, const),
            pl.BlockSpec(wft.shape, const),
            pl.BlockSpec(bf.shape, const),
            pl.BlockSpec(cw.shape, const),
            pl.BlockSpec(tri.shape, const),
        ],
        out_specs=[
            pl.BlockSpec((rows, D_ATTN), row_map),
            pl.BlockSpec((1, D_ATTN, rows), seq_map),
            pl.BlockSpec((1, D_ATTN, rows), seq_map),
            pl.BlockSpec((1, N_HEADS, rows), seq_map),
            pl.BlockSpec((1, N_HEADS, rows), seq_map),
            pl.BlockSpec((rows, D_CONV), row_map),
            pl.BlockSpec((1, CONV_W - 1, D_CONV), lambda b, s: (b, 0, 0)),
        ],
        out_shape=[
            jax.ShapeDtypeStruct((batch * seq, D_ATTN), BF16),
            jax.ShapeDtypeStruct((batch, D_ATTN, seq), F32),
            jax.ShapeDtypeStruct((batch, D_ATTN, seq), F32),
            jax.ShapeDtypeStruct((batch, N_HEADS, seq), F32),
            jax.ShapeDtypeStruct((batch, N_HEADS, seq), F32),
            jax.ShapeDtypeStruct((batch * seq, D_CONV), BF16),
            jax.ShapeDtypeStruct((batch, CONV_W - 1, D_CONV), F32),
        ],
        scratch_shapes=[
            pltpu.VMEM((SUBLANES, D_CONV), F32),
            pltpu.VMEM((N_HEADS, 1), F32),
        ],
        compiler_params=pltpu.CompilerParams(
            dimension_semantics=("arbitrary", "arbitrary"), vmem_limit_bytes=VMEM_LIMIT),
        name="prompt_proj",
    )(x, wn, wkvt, wft, bf, cw, tri)


def _attn_kernel(q_ref, kt_ref, vt_ref, c_ref, o_ref, kb_ref, vb_ref):
    pair = pl.program_id(1)
    qi = pl.program_id(2)
    tq = q_ref.shape[0]
    tk = tq
    seq = kt_ref.shape[2]

    @pl.when(qi == 0)
    def _():
        kt = kt_ref[0]
        vt = vt_ref[0]
        rowi = lax.broadcasted_iota(jnp.int32, (HEAD_DIM, seq), 0)
        ones_row = jnp.where(rowi == 0, 1.0, 0.0).astype(BF16)
        extra = []
        for hh in range(2):
            hi, mid, lo = _split3(-c_ref[0, pl.ds(2 * pair + hh, 1), :])
            e = jnp.where(rowi == 0, hi.astype(F32),
                          jnp.where(rowi == 1, mid.astype(F32),
                                    jnp.where(rowi == 2, lo.astype(F32), 0.0)))
            extra.append(e.astype(BF16))
        kb_ref[0, 0:HEAD_DIM] = kt[0:HEAD_DIM].astype(BF16)
        kb_ref[0, HEAD_DIM:] = extra[0]
        kb_ref[1, 0:HEAD_DIM] = extra[1]
        kb_ref[1, HEAD_DIM:] = kt[HEAD_DIM:].astype(BF16)
        vb_ref[0, 0:HEAD_DIM] = vt[0:HEAD_DIM].astype(BF16)
        vb_ref[0, HEAD_DIM:] = ones_row
        vb_ref[1, 0:HEAD_DIM] = ones_row
        vb_ref[1, HEAD_DIM:] = vt[HEAD_DIM:].astype(BF16)

    q2 = q_ref[...]
    lane = lax.broadcasted_iota(jnp.int32, (1, LANES), 1)
    causal = (lax.broadcasted_iota(jnp.int32, (tq, tk), 1)
              <= lax.broadcasted_iota(jnp.int32, (tq, tk), 0))

    qa = []
    for hh in range(2):
        base = (1 - hh) * HEAD_DIM
        ones3 = jnp.where((lane >= base) & (lane < base + 3), 1.0, 0.0).astype(BF16)
        qa.append(jnp.where((lane // HEAD_DIM) == hh, q2, ones3))

    def step(j, carry, masked):
        col = pl.multiple_of(j * tk, tk)
        if masked:
            half = tq // 2
            chains = [(hh, slice(r, r + half), r + half) for hh in range(2) for r in (0, half)]
        else:
            chains = [(hh, slice(0, tq), tk) for hh in range(2)]

        def logits(hh, rows, ncols):
            return _dot(qa[hh][rows], kb_ref[hh, :, pl.ds(col, ncols)])

        ms, accs = ([], []), ([], [])
        s_next = logits(*chains[0])
        for n, (hh, rows, ncols) in enumerate(chains):
            s = s_next
            if n + 1 < len(chains):
                s_next = logits(*chains[n + 1])
            m, acc = carry[hh]
            if masked:
                s = jnp.where(causal[rows, 0:ncols], s, -jnp.inf)
            m_new = jnp.maximum(m[rows], jnp.max(s, axis=-1, keepdims=True))
            alpha = jnp.exp(m[rows] - m_new)
            p = jnp.exp(s - m_new).astype(BF16)
            pv = _dot_nt(p, vb_ref[hh, :, pl.ds(col, ncols)])
            ms[hh].append(m_new)
            accs[hh].append(alpha * acc[rows] + pv)
        return tuple((jnp.concatenate(ms[hh], axis=0), jnp.concatenate(accs[hh], axis=0))
                     for hh in range(2))

    init = tuple((jnp.full((tq, 1), -jnp.inf, F32), jnp.zeros((tq, LANES), F32)) for _ in range(2))
    carry = lax.fori_loop(0, qi, functools.partial(step, masked=False), init)
    (_, acc0), (_, acc1) = step(qi, carry, True)
    out0 = acc0 / acc0[:, HEAD_DIM:HEAD_DIM + 1]
    out1 = acc1 / acc1[:, 0:1]
    o_ref[...] = jnp.where((lane // HEAD_DIM) == 0, out0, out1).astype(BF16)


def _prompt_attn(q, kt, vt, c, batch, seq):
    t = ATTN_TILE
    nq = seq // t
    pairs = D_ATTN // LANES
    return pl.pallas_call(
        _attn_kernel,
        grid=(batch, pairs, nq),
        in_specs=[
            pl.BlockSpec((t, LANES), lambda b, p, i: (b * nq + i, p)),
            pl.BlockSpec((1, LANES, seq), lambda b, p, i: (b, p, 0)),
            pl.BlockSpec((1, LANES, seq), lambda b, p, i: (b, p, 0)),
            pl.BlockSpec((1, N_HEADS, seq), lambda b, p, i: (b, 0, 0)),
        ],
        out_specs=pl.BlockSpec((t, LANES), lambda b, p, i: (b * nq + i, p)),
        out_shape=jax.ShapeDtypeStruct((batch * seq, D_ATTN), BF16),
        scratch_shapes=[pltpu.VMEM((2, LANES, seq), BF16), pltpu.VMEM((2, LANES, seq), BF16)],
        compiler_params=pltpu.CompilerParams(
            dimension_semantics=("arbitrary", "arbitrary", "arbitrary"),
            vmem_limit_bytes=VMEM_LIMIT),
        name="prompt_attn",
    )(q, kt, vt, c)


def _mix_kernel(alpha, x_ref, a_ref, cb_ref, wg_ref, bg_ref, wpa_ref, wpc_ref, wo_ref,
                g_ref, b_ref, o_ref, m_ref):
    subs = _row_subtiles(x_ref.shape[0])
    for rows in subs:
        xb = x_ref[rows].astype(BF16)
        a = a_ref[rows]
        cb = cb_ref[rows]
        for c in range(0, D_MODEL, MIX_COLS):
            cols = slice(c, c + MIX_COLS)
            gcols = slice(D_MODEL + c, D_MODEL + c + MIX_COLS)
            ga = jax.nn.sigmoid(_dot(xb, wg_ref[:, cols]) + bg_ref[:, cols])
            gc = jax.nn.sigmoid(_dot(xb, wg_ref[:, gcols]) + bg_ref[:, gcols])
            m = ga * _dot(a, wpa_ref[:, cols]) + gc * _dot(cb, wpc_ref[:, cols])
            m_ref[rows, cols] = m.astype(BF16)
    for rows in subs:
        tm = _dot(m_ref[rows], wo_ref[...])
        o_ref[rows] = _layer_norm(alpha * x_ref[rows] + tm, g_ref[...], b_ref[...])


def _mix(layer, x, a, cb, wg, bg, wpa, wpc, wo, g, b, alpha, rows):
    n = x.shape[0]
    row_map = lambda i: (i, 0)
    return pl.pallas_call(
        functools.partial(_mix_kernel, alpha),
        grid=(n // rows,),
        in_specs=[
            pl.BlockSpec((rows, D_MODEL), row_map),
            pl.BlockSpec((rows, D_ATTN), row_map),
            pl.BlockSpec((rows, D_CONV), row_map),
            _resident(wg), _resident(bg),
            _resident(wpa, layer), _resident(wpc, layer), _resident(wo, layer),
            _resident(g), _resident(b),
        ],
        out_specs=pl.BlockSpec((rows, D_MODEL), row_map),
        out_shape=jax.ShapeDtypeStruct((n, D_MODEL), F32),
        scratch_shapes=[pltpu.VMEM((rows, D_MODEL), BF16)],
        compiler_params=pltpu.CompilerParams(
            dimension_semantics=("arbitrary",), vmem_limit_bytes=VMEM_LIMIT),
        name="mix",
    )(x, a, cb, wg, bg, wpa, wpc, wo, g, b)


def _ffn_body(alpha, x_ref, wgu_ref, wd_ref, g_ref, b_ref, o_ref, act_ref):
    subs = _row_subtiles(x_ref.shape[0])
    for rows in subs:
        xb = x_ref[rows].astype(BF16)
        for c in range(0, D_FF, FFN_COLS):
            gate = _dot(xb, wgu_ref[:, c:c + FFN_COLS])
            up = _dot(xb, wgu_ref[:, D_FF + c:D_FF + c + FFN_COLS])
            act_ref[rows, c:c + FFN_COLS] = (jax.nn.silu(gate) * up).astype(BF16)
    for rows in subs:
        f = _dot(act_ref[rows], wd_ref[...])
        o_ref[rows] = _layer_norm(alpha * x_ref[rows] + f, g_ref[...], b_ref[...])


def _resident(a, layer=None):
    if layer is None:
        return pl.BlockSpec(a.shape, lambda *_: (0,) * a.ndim, pipeline_mode=pl.Buffered(1))
    return pl.BlockSpec((None,) + a.shape[1:], lambda *_: (layer,) + (0,) * (a.ndim - 1),
                        pipeline_mode=pl.Buffered(1))


def _ffn(layer, x, wgu, wd, g, b, alpha, rows):
    n = x.shape[0]
    return pl.pallas_call(
        functools.partial(_ffn_body, alpha),
        grid=(n // rows,),
        in_specs=[
            pl.BlockSpec((rows, D_MODEL), lambda i: (i, 0)),
            _resident(wgu, layer), _resident(wd, layer), _resident(g), _resident(b),
        ],
        out_specs=pl.BlockSpec((rows, D_MODEL), lambda i: (i, 0)),
        out_shape=jax.ShapeDtypeStruct((n, D_MODEL), F32),
        scratch_shapes=[pltpu.VMEM((rows, D_FF), BF16)],
        compiler_params=pltpu.CompilerParams(
            dimension_semantics=("arbitrary",), vmem_limit_bytes=VMEM_LIMIT),
        name="ffn",
    )(x, wgu, wd, g, b)


def _sample_proj_kernel(x_ref, wn_ref, wkv_ref, wf_ref, bf_ref, cw_ref, h0_ref, h1_ref,
                        q_ref, k_ref, v_ref, lf_ref, cb_ref, u_ref):
    xb = x_ref[...].astype(BF16)
    z = _dot(xb, wn_ref[...])
    q_ref[...] = z[:, 0:D_ATTN] * SCALE
    kv = _dot_nt(xb, wkv_ref[...])
    k_ref[...] = kv[:, 0:D_ATTN]
    v_ref[...] = kv[:, D_ATTN:2 * D_ATTN]
    lf_ref[...] = jax.nn.log_sigmoid(_dot_nt(wf_ref[...], xb) + bf_ref[...])
    h = z[:, D_ATTN:D_ATTN + D_CONV]
    gb = z[:, D_ATTN + D_CONV:D_ATTN + 2 * D_CONV]
    gc = z[:, D_ATTN + 2 * D_CONV:D_ATTN + 3 * D_CONV]
    u = gc * h
    cw = cw_ref[...]
    cv = cw[0:1] * h0_ref[...] + cw[1:2] * h1_ref[...] + cw[2:3] * u
    cb_ref[...] = (gb * cv).astype(BF16)
    u_ref[...] = u


def _sample_proj(x, wn, wkvt, wft, bf, cw, h0, h1):
    n = x.shape[0]
    args = (x, wn, wkvt, wft, bf, cw, h0, h1)
    full = lambda a: pl.BlockSpec(a.shape, lambda i: (0,) * a.ndim)
    out_shape = [
        jax.ShapeDtypeStruct((n, D_ATTN), F32),
        jax.ShapeDtypeStruct((n, D_ATTN), F32),
        jax.ShapeDtypeStruct((n, D_ATTN), F32),
        jax.ShapeDtypeStruct((N_HEADS, n), F32),
        jax.ShapeDtypeStruct((n, D_CONV), BF16),
        jax.ShapeDtypeStruct((n, D_CONV), F32),
    ]
    return pl.pallas_call(
        _sample_proj_kernel,
        grid=(1,),
        in_specs=[full(a) for a in args],
        out_specs=[full(s) for s in out_shape],
        out_shape=out_shape,
        compiler_params=pltpu.CompilerParams(
            dimension_semantics=("arbitrary",), vmem_limit_bytes=VMEM_LIMIT),
        name="sample_proj",
    )(*args)


def _head_rows(h):
    return slice(h * HEAD_DIM, (h + 1) * HEAD_DIM)


def _paged_logits(h, qb_ref, k_refs, s_ref):
    qh = qb_ref[_head_rows(h), :]
    for j, k in enumerate(k_refs):
        prod = k[0, 0, _head_rows(h), :] * qh
        s_ref[h:h + 1, j * LANES:(j + 1) * LANES] = jnp.sum(prod, axis=0, keepdims=True)


def _paged_softmax(s_ref, p_ref, a_ref, lf_refs, tri_ref, m_ref, l_ref, carry_ref):
    pages = len(lf_refs)
    m = m_ref[...]
    carry = carry_ref[...]
    lf = jnp.concatenate([r[0, 0] for r in lf_refs], axis=0)
    n = pages * N_HEADS
    r = _dot(jnp.concatenate(_split3(lf), axis=0), tri_ref[...])
    r = r[0:n] + r[n:2 * n] + r[2 * n:3 * n]
    cs = r[:, 0:LANES]
    tot = r[:, LANES:2 * LANES]
    c_pages = []
    for j in range(pages):
        c_pages.append(cs[j * N_HEADS:(j + 1) * N_HEADS] + carry)
        carry = carry + tot[j * N_HEADS:(j + 1) * N_HEADS]
    s = s_ref[...] - jnp.concatenate(c_pages, axis=1)
    m_new = jnp.maximum(m, jnp.max(s, axis=-1, keepdims=True))
    alpha = jnp.exp(m - m_new)
    p = jnp.exp(s - m_new)
    p_ref[...] = p
    a_ref[...] = alpha
    m_ref[...] = m_new
    l_ref[...] = alpha * l_ref[...] + jnp.sum(p, axis=-1, keepdims=True)
    carry_ref[...] = carry


def _paged_values(h, p_ref, a_ref, v_refs, acc_ref):
    acc = acc_ref[_head_rows(h), :] * a_ref[h:h + 1, :]
    for j, v in enumerate(v_refs):
        acc = acc + v[0, 0, _head_rows(h), :] * p_ref[h:h + 1, j * LANES:(j + 1) * LANES]
    acc_ref[_head_rows(h), :] = acc


def _paged_finish(qc_ref, knc_ref, vnc_ref, lfn_ref, m_ref, l_ref, acc_ref, carry_ref, o_ref):
    m = m_ref[...]
    qk = qc_ref[0] * knc_ref[0]
    s_new = jnp.concatenate([jnp.sum(qk[_head_rows(h)], axis=0, keepdims=True)
                             for h in range(N_HEADS)], axis=0)
    s_new = s_new - (carry_ref[:, 0:1] + lfn_ref[0])
    m_fin = jnp.maximum(m, s_new)
    alpha = jnp.exp(m - m_fin)
    p_new = jnp.exp(s_new - m_fin)
    l_fin = alpha * l_ref[...] + p_new
    for h in range(N_HEADS):
        rows = _head_rows(h)
        num = (alpha[h:h + 1, :] * jnp.sum(acc_ref[rows, :], axis=-1, keepdims=True)
               + p_new[h:h + 1, :] * vnc_ref[0, rows, :])
        o_ref[0, rows, :] = num / l_fin[h:h + 1, :]


def _ffn_paged_kernel(alpha, pages, pt_ref, x_ref, wgu_ref, wd_ref, g_ref, b_ref,
                      q_ref, kn_ref, vn_ref, lfn_ref, tri_ref, *refs):
    k_refs = refs[0:pages]
    v_refs = refs[pages:2 * pages]
    lf_refs = refs[2 * pages:3 * pages]
    y_ref, o_ref = refs[3 * pages:3 * pages + 2]
    (act_ref, part_ref, qb_ref, s_ref, p_ref, a_ref,
     m_ref, l_ref, acc_ref, carry_ref) = refs[3 * pages + 2:]
    quarter = pl.program_id(1)
    subs = _row_subtiles(x_ref.shape[0])

    def gate_up(c0, c1):
        def piece(rows, c):
            xb = x_ref[rows].astype(BF16)
            gate = _dot(xb, wgu_ref[:, c:c + FFN_COLS])
            up = _dot(xb, wgu_ref[:, D_FF + c:D_FF + c + FFN_COLS])
            act_ref[rows, c:c + FFN_COLS] = (jax.nn.silu(gate) * up).astype(BF16)
        return [functools.partial(piece, rows, c)
                for rows in subs for c in range(c0, c1, FFN_COLS)]

    def down(k0, k1, first=False, last=False):
        def piece(rows):
            part = _dot(act_ref[rows, k0:k1], wd_ref[k0:k1, :])
            if first:
                part_ref[rows] = part
            elif not last:
                part_ref[rows] += part
            else:
                y_ref[rows] = _layer_norm(alpha * x_ref[rows] + (part_ref[rows] + part),
                                          g_ref[...], b_ref[...])
        return [functools.partial(piece, rows) for rows in subs]

    c2, c4 = 4 * FFN_COLS, 8 * FFN_COLS
    ffn_work = [
        gate_up(0, c2),
        gate_up(c2, c4),
        gate_up(c4, D_FF) + down(0, c2, first=True),
        down(c2, c4) + down(c4, D_FF, last=True),
    ]
    attn_work = (
        [functools.partial(_paged_logits, h, qb_ref, k_refs, s_ref) for h in range(N_HEADS)]
        + [functools.partial(_paged_softmax, s_ref, p_ref, a_ref, lf_refs, tri_ref,
                             m_ref, l_ref, carry_ref)]
        + [functools.partial(_paged_values, h, p_ref, a_ref, v_refs, acc_ref)
           for h in range(N_HEADS)])
    for k, pieces in enumerate(ffn_work):
        @pl.when(quarter == k)
        def _(k=k, pieces=pieces):
            if k == 0:
                m_ref[...] = jnp.full_like(m_ref, -jnp.inf)
                l_ref[...] = jnp.zeros_like(l_ref)
                acc_ref[...] = jnp.zeros_like(acc_ref)
                carry_ref[...] = jnp.zeros_like(carry_ref)
                qb_ref[...] = jnp.broadcast_to(q_ref[0], qb_ref.shape)
            for piece in attn_work + pieces:
                piece()
            if k == len(ffn_work) - 1:
                _paged_finish(q_ref, kn_ref, vn_ref, lfn_ref, m_ref, l_ref, acc_ref, carry_ref,
                              o_ref)


def _ffn_paged(layer, page_table, x, wgu, wd, g, b, alpha, q, kn, vn, lfn, tri,
               cache_kt, cache_vt, cache_lft):
    n, n_pages = page_table.shape
    quarters = 4
    assert n_pages % quarters == 0 and x.shape[0] % n == 0
    pages = n_pages // quarters
    rows = x.shape[0] // n
    assert rows % SUBLANES == 0
    x_spec = pl.BlockSpec((rows, D_MODEL), lambda s, k, pt: (s, 0))
    row = pl.BlockSpec((1, D_ATTN, 1), lambda s, k, pt: (s, 0, 0))

    def page_spec(shape, j):
        return pl.BlockSpec((1, 1) + shape,
                            lambda s, k, pt: (layer, pt[s, k * pages + j], 0, 0))

    in_specs = [x_spec, _resident(wgu, layer), _resident(wd, layer), _resident(g), _resident(b),
                row, row, row,
                pl.BlockSpec((1, N_HEADS, 1), lambda s, k, pt: (s, 0, 0)),
                _resident(tri)]
    in_specs += [page_spec((D_ATTN, PAGE_SIZE), j) for j in range(pages)]
    in_specs += [page_spec((D_ATTN, PAGE_SIZE), j) for j in range(pages)]
    in_specs += [page_spec((N_HEADS, PAGE_SIZE), j) for j in range(pages)]
    grid_spec = pltpu.PrefetchScalarGridSpec(
        num_scalar_prefetch=1,
        grid=(n, quarters),
        in_specs=in_specs,
        out_specs=[x_spec, row],
        scratch_shapes=[
            pltpu.VMEM((rows, D_FF), BF16),
            pltpu.VMEM((rows, D_MODEL), F32),
            pltpu.VMEM((D_ATTN, LANES), F32),
            pltpu.VMEM((N_HEADS, pages * PAGE_SIZE), F32),
            pltpu.VMEM((N_HEADS, pages * PAGE_SIZE), F32),
            pltpu.VMEM((N_HEADS, 1), F32),
            pltpu.VMEM((N_HEADS, 1), F32),
            pltpu.VMEM((N_HEADS, 1), F32),
            pltpu.VMEM((D_ATTN, LANES), F32),
            pltpu.VMEM((N_HEADS, LANES), F32),
        ],
    )
    return pl.pallas_call(
        functools.partial(_ffn_paged_kernel, alpha, pages),
        grid_spec=grid_spec,
        out_shape=[jax.ShapeDtypeStruct(x.shape, F32),
                   jax.ShapeDtypeStruct((n, D_ATTN, 1), F32)],
        compiler_params=pltpu.CompilerParams(
            dimension_semantics=("arbitrary", "arbitrary"), vmem_limit_bytes=VMEM_LIMIT),
        name="ffn_paged",
    )(page_table, x, wgu, wd, g, b, q, kn, vn, lfn, tri,
      *([cache_kt] * pages), *([cache_vt] * pages), *([cache_lft] * pages))


def kernel(x_prompt, x_sample, cache_k, cache_v, cache_logf, state_conv, page_table, w_in, b_f, b_gate, conv_w, w_attn_proj, w_conv_proj, w_out, ln1_g, ln1_b, w_gate_up, w_down, ln2_g, ln2_b):
    batch, seq, _ = x_prompt.shape
    n_dec = x_sample.shape[0]
    depth = w_in.shape[0]
    n_pool = cache_k.shape[1]
    alpha = (2 * depth) ** 0.25

    cache_kt = jnp.transpose(cache_k, (0, 1, 3, 4, 2)).reshape(depth, n_pool, D_ATTN, PAGE_SIZE)
    cache_vt = jnp.transpose(cache_v, (0, 1, 3, 4, 2)).reshape(depth, n_pool, D_ATTN, PAGE_SIZE)
    cache_lft = jnp.transpose(cache_logf, (0, 1, 3, 2))
    tri = jnp.triu(jnp.ones((LANES, LANES), F32)).astype(BF16)
    tri_ones = jnp.concatenate([tri, jnp.ones((LANES, LANES), BF16)], axis=1)

    xp = x_prompt.reshape(batch * seq, D_MODEL)
    xs = x_sample.reshape(n_dec, D_MODEL)
    kp, vp, lp, cp, ksm, vsm, lsm, csm = [], [], [], [], [], [], [], []
    wpa, wpc, wo = w_attn_proj.astype(BF16), w_conv_proj.astype(BF16), w_out.astype(BF16)
    wgu, wd = w_gate_up.astype(BF16), w_down.astype(BF16)
    for l in range(depth):
        wt = jnp.transpose(w_in[l])
        wn = jnp.concatenate([w_in[l][:, OFF_Q:OFF_K], w_in[l][:, OFF_H:OFF_G]], axis=1).astype(BF16)
        wkvt = wt[OFF_K:OFF_F].astype(BF16)
        wft = wt[OFF_F:OFF_H].astype(BF16)
        wg = w_in[l][:, OFF_G:].astype(BF16)
        bf = b_f[l].reshape(N_HEADS, 1)
        bg = b_gate[l].reshape(1, 2 * D_MODEL)
        cw = conv_w[l]
        g1, b1 = ln1_g[l].reshape(1, D_MODEL), ln1_b[l].reshape(1, D_MODEL)
        g2, b2 = ln2_g[l].reshape(1, D_MODEL), ln2_b[l].reshape(1, D_MODEL)

        q, kt, vt, lf, c, cb, st = _prompt_proj(xp, wn, wkvt, wft, bf, cw, tri, batch, seq)
        a = _prompt_attn(q, kt, vt, c, batch, seq)
        x1 = _mix(l, xp, a, cb, wg, bg, wpa, wpc, wo, g1, b1, alpha, MIX_ROWS)
        kp.append(kt); vp.append(vt); lp.append(lf); cp.append(st)
        h0, h1 = state_conv[l, :, 0, :], state_conv[l, :, 1, :]
        qs, ks, vs, lfs, cbs, us = _sample_proj(xs, wn, wkvt, wft, bf, cw, h0, h1)

        xp, a_s = _ffn_paged(l, page_table, x1, wgu, wd, g2, b2, alpha,
                             qs.reshape(n_dec, D_ATTN, 1), ks.reshape(n_dec, D_ATTN, 1),
                             vs.reshape(n_dec, D_ATTN, 1),
                             jnp.transpose(lfs).reshape(n_dec, N_HEADS, 1),
                             tri_ones, cache_kt, cache_vt, cache_lft)

        x1s = _mix(l, xs, a_s.reshape(n_dec, D_ATTN).astype(BF16), cbs, wg, bg, wpa, wpc, wo,
                   g1, b1, alpha, n_dec)
        xs = _ffn(l, x1s, wgu, wd, g2, b2, alpha, n_dec)
        ksm.append(ks); vsm.append(vs); lsm.append(lfs); csm.append(jnp.stack([h1, us], axis=1))

    def heads_last(t):
        return jnp.transpose(t.reshape(depth, batch, N_HEADS, HEAD_DIM, seq), (0, 1, 4, 2, 3))

    y_prompt = xp.reshape(batch, seq, D_MODEL)
    y_sample = xs.reshape(n_dec, 1, D_MODEL)
    k_prompt = heads_last(jnp.stack(kp))
    v_prompt = heads_last(jnp.stack(vp))
    logf_prompt = jnp.transpose(jnp.stack(lp), (0, 1, 3, 2))
    conv_prompt = jnp.stack(cp)
    k_sample = jnp.stack(ksm).reshape(depth, n_dec, 1, N_HEADS, HEAD_DIM)
    v_sample = jnp.stack(vsm).reshape(depth, n_dec, 1, N_HEADS, HEAD_DIM)
    logf_sample = jnp.transpose(jnp.stack(lsm), (0, 2, 1)).reshape(depth, n_dec, 1, N_HEADS)
    conv_sample = jnp.stack(csm)
    return (y_prompt, y_sample, k_prompt, v_prompt, logf_prompt, conv_prompt,
            k_sample, v_sample, logf_sample, conv_sample)
```

```python
import functools

import jax
import jax.numpy as jnp
from jax import lax
from jax.experimental import pallas as pl
from jax.experimental.pallas import tpu as pltpu

D_MODEL = 1024
N_HEADS = 8
HEAD_DIM = 64
D_ATTN = N_HEADS * HEAD_DIM
D_CONV = 512
CONV_W = 3
D_FF = 2816
PAGE_SIZE = 128
LN_EPS = 1e-5
SCALE = HEAD_DIM ** -0.5

OFF_Q = 0
OFF_K = OFF_Q + D_ATTN
OFF_V = OFF_K + D_ATTN
OFF_F = OFF_V + D_ATTN
OFF_H = OFF_F + N_HEADS
OFF_B = OFF_H + D_CONV
OFF_C = OFF_B + D_CONV
OFF_G = OFF_C + D_CONV

LANES = 128
SUBLANES = 8
VMEM_LIMIT = 56 * 1024 * 1024

PROJ_ROWS = 1024
PROJ_COLS = 256
ATTN_TILE = 1024
MIX_ROWS = 1024
MIX_COLS = 256
SUB_ROWS = 256
FFN_COLS = 256

BF16 = jnp.bfloat16
F32 = jnp.float32

_NT = (((1,), (1,)), ((), ()))


def _dot(a, b):
    return jnp.dot(a, b, preferred_element_type=F32)


def _dot_nt(a, b):
    return lax.dot_general(a, b, _NT, preferred_element_type=F32)


def _split3(x):
    hi = x.astype(BF16)
    r1 = x - hi.astype(F32)
    mid = r1.astype(BF16)
    lo = (r1 - mid.astype(F32)).astype(BF16)
    return hi, mid, lo


def _lane_cumsum(x, tri):
    rows = x.shape[0]
    parts = jnp.concatenate(_split3(x), axis=0)
    cs = _dot(parts, tri)
    return cs[:rows] + cs[rows:2 * rows] + cs[2 * rows:]


def _row_subtiles(rows):
    sub = min(rows, SUB_ROWS)
    return [slice(r, r + sub) for r in range(0, rows, sub)]


def _layer_norm(y, g, b):
    mu = jnp.mean(y, axis=-1, keepdims=True)
    d = y - mu
    var = jnp.mean(d * d, axis=-1, keepdims=True)
    return d * lax.rsqrt(var + LN_EPS) * g + b


def _proj_kernel(x_ref, wn_ref, wkv_ref, wf_ref, bf_ref, cw_ref, tri_ref,
                 q_ref, kt_ref, vt_ref, lf_ref, c_ref, cb_ref, st_ref,
                 hist_ref, carry_ref):
    si = pl.program_id(1)
    rows = x_ref.shape[0]

    @pl.when(si == 0)
    def _():
        hist_ref[...] = jnp.zeros_like(hist_ref)
        carry_ref[...] = jnp.zeros_like(carry_ref)

    xb = x_ref[...].astype(BF16)
    pc = PROJ_COLS
    for c in range(0, D_ATTN, pc):
        q_ref[:, c:c + pc] = (_dot(xb, wn_ref[:, c:c + pc]) * SCALE).astype(BF16)
    for r in range(0, D_ATTN, pc):
        kt_ref[0, r:r + pc] = _dot_nt(wkv_ref[r:r + pc], xb)
        vt_ref[0, r:r + pc] = _dot_nt(wkv_ref[D_ATTN + r:D_ATTN + r + pc], xb)

    lf = jax.nn.log_sigmoid(_dot_nt(wf_ref[...], xb) + bf_ref[...])
    lf_ref[0] = lf
    carry = carry_ref[...]
    for j in range(rows // LANES):
        cs = _lane_cumsum(lf[:, j * LANES:(j + 1) * LANES], tri_ref[...]) + carry
        c_ref[0, :, j * LANES:(j + 1) * LANES] = cs
        carry = cs[:, LANES - 1:LANES]
    carry_ref[...] = carry

    row = lax.broadcasted_iota(jnp.int32, (SUBLANES, pc), 0)
    for c in range(0, D_CONV, pc):
        h = _dot(xb, wn_ref[:, D_ATTN + c:D_ATTN + c + pc])
        gb = _dot(xb, wn_ref[:, D_ATTN + D_CONV + c:D_ATTN + D_CONV + c + pc])
        gc = _dot(xb, wn_ref[:, D_ATTN + 2 * D_CONV + c:D_ATTN + 2 * D_CONV + c + pc])
        u = gc * h
        hist = hist_ref[:, c:c + pc]
        r1 = pltpu.roll(u, 1, 0)
        r2 = pltpu.roll(u, 2, 0)
        h1 = pltpu.roll(hist, 1, 0)
        h2 = pltpu.roll(hist, 2, 0)
        u1 = jnp.concatenate([jnp.where(row < 1, h1, r1[0:SUBLANES]), r1[SUBLANES:]], axis=0)
        u2 = jnp.concatenate([jnp.where(row < 2, h2, r2[0:SUBLANES]), r2[SUBLANES:]], axis=0)
        cw = cw_ref[:, c:c + pc]
        cv = cw[0:1] * u2 + cw[1:2] * u1 + cw[2:3] * u
        cb_ref[:, c:c + pc] = (gb * cv).astype(BF16)
        hist_ref[:, c:c + pc] = u[rows - SUBLANES:rows]

    @pl.when(si == pl.num_programs(1) - 1)
    def _():
        st_ref[0] = hist_ref[SUBLANES - (CONV_W - 1):SUBLANES, :]


def _prompt_proj(x, wn, wkvt, wft, bf, cw, tri, batch, seq):
    rows = PROJ_ROWS
    ns = seq // rows
    row_map = lambda b, s: (b * ns + s, 0)
    const = lambda b, s: (0, 0)
    seq_map = lambda b, s: (b, 0, s)
    return pl.pallas_call(
        _proj_kernel,
        grid=(batch, ns),
        in_specs=[
            pl.BlockSpec((rows, D_MODEL), row_map),
            pl.BlockSpec(wn.shape, const),
            pl.BlockSpec(wkvt.shape, const),
            pl.BlockSpec(wft.shape, const),
            pl.BlockSpec(bf.shape, const),
            pl.BlockSpec(cw.shape, const),
            pl.BlockSpec(tri.shape, const),
        ],
        out_specs=[
            pl.BlockSpec((rows, D_ATTN), row_map),
            pl.BlockSpec((1, D_ATTN, rows), seq_map),
            pl.BlockSpec((1, D_ATTN, rows), seq_map),
            pl.BlockSpec((1, N_HEADS, rows), seq_map),
            pl.BlockSpec((1, N_HEADS, rows), seq_map),
            pl.BlockSpec((rows, D_CONV), row_map),
            pl.BlockSpec((1, CONV_W - 1, D_CONV), lambda b, s: (b, 0, 0)),
        ],
        out_shape=[
            jax.ShapeDtypeStruct((batch * seq, D_ATTN), BF16),
            jax.ShapeDtypeStruct((batch, D_ATTN, seq), F32),
            jax.ShapeDtypeStruct((batch, D_ATTN, seq), F32),
            jax.ShapeDtypeStruct((batch, N_HEADS, seq), F32),
            jax.ShapeDtypeStruct((batch, N_HEADS, seq), F32),
            jax.ShapeDtypeStruct((batch * seq, D_CONV), BF16),
            jax.ShapeDtypeStruct((batch, CONV_W - 1, D_CONV), F32),
        ],
        scratch_shapes=[
            pltpu.VMEM((SUBLANES, D_CONV), F32),
            pltpu.VMEM((N_HEADS, 1), F32),
        ],
        compiler_params=pltpu.CompilerParams(
            dimension_semantics=("arbitrary", "arbitrary"), vmem_limit_bytes=VMEM_LIMIT),
        name="prompt_proj",
    )(x, wn, wkvt, wft, bf, cw, tri)


def _attn_kernel(q_ref, kt_ref, vt_ref, c_ref, o_ref, kb_ref, vb_ref):
    pair = pl.program_id(1)
    qi = pl.program_id(2)
    tq = q_ref.shape[0]
    tk = tq
    seq = kt_ref.shape[2]

    @pl.when(qi == 0)
    def _():
        kt = kt_ref[0]
        vt = vt_ref[0]
        rowi = lax.broadcasted_iota(jnp.int32, (HEAD_DIM, seq), 0)
        ones_row = jnp.where(rowi == 0, 1.0, 0.0).astype(BF16)
        extra = []
        for hh in range(2):
            hi, mid, lo = _split3(-c_ref[0, pl.ds(2 * pair + hh, 1), :])
            e = jnp.where(rowi == 0, hi.astype(F32),
                          jnp.where(rowi == 1, mid.astype(F32),
                                    jnp.where(rowi == 2, lo.astype(F32), 0.0)))
            extra.append(e.astype(BF16))
        kb_ref[0, 0:HEAD_DIM] = kt[0:HEAD_DIM].astype(BF16)
        kb_ref[0, HEAD_DIM:] = extra[0]
        kb_ref[1, 0:HEAD_DIM] = extra[1]
        kb_ref[1, HEAD_DIM:] = kt[HEAD_DIM:].astype(BF16)
        vb_ref[0, 0:HEAD_DIM] = vt[0:HEAD_DIM].astype(BF16)
        vb_ref[0, HEAD_DIM:] = ones_row
        vb_ref[1, 0:HEAD_DIM] = ones_row
        vb_ref[1, HEAD_DIM:] = vt[HEAD_DIM:].astype(BF16)

    q2 = q_ref[...]
    lane = lax.broadcasted_iota(jnp.int32, (1, LANES), 1)
    causal = (lax.broadcasted_iota(jnp.int32, (tq, tk), 1)
              <= lax.broadcasted_iota(jnp.int32, (tq, tk), 0))

    qa = []
    for hh in range(2):
        base = (1 - hh) * HEAD_DIM
        ones3 = jnp.where((lane >= base) & (lane < base + 3), 1.0, 0.0).astype(BF16)
        qa.append(jnp.where((lane // HEAD_DIM) == hh, q2, ones3))

    def step(j, carry, masked):
        col = pl.multiple_of(j * tk, tk)
        if masked:
            half = tq // 2
            chains = [(hh, slice(r, r + half), r + half) for hh in range(2) for r in (0, half)]
        else:
            chains = [(hh, slice(0, tq), tk) for hh in range(2)]

        def logits(hh, rows, ncols):
            return _dot(qa[hh][rows], kb_ref[hh, :, pl.ds(col, ncols)])

        ms, accs = ([], []), ([], [])
        s_next = logits(*chains[0])
        for n, (hh, rows, ncols) in enumerate(chains):
            s = s_next
            if n + 1 < len(chains):
                s_next = logits(*chains[n + 1])
            m, acc = carry[hh]
            if masked:
                s = jnp.where(causal[rows, 0:ncols], s, -jnp.inf)
            m_new = jnp.maximum(m[rows], jnp.max(s, axis=-1, keepdims=True))
            alpha = jnp.exp(m[rows] - m_new)
            p = jnp.exp(s - m_new).astype(BF16)
            pv = _dot_nt(p, vb_ref[hh, :, pl.ds(col, ncols)])
            ms[hh].append(m_new)
            accs[hh].append(alpha * acc[rows] + pv)
        return tuple((jnp.concatenate(ms[hh], axis=0), jnp.concatenate(accs[hh], axis=0))
                     for hh in range(2))

    init = tuple((jnp.full((tq, 1), -jnp.inf, F32), jnp.zeros((tq, LANES), F32)) for _ in range(2))
    carry = lax.fori_loop(0, qi, functools.partial(step, masked=False), init)
    (_, acc0), (_, acc1) = step(qi, carry, True)
    out0 = acc0 / acc0[:, HEAD_DIM:HEAD_DIM + 1]
    out1 = acc1 / acc1[:, 0:1]
    o_ref[...] = jnp.where((lane // HEAD_DIM) == 0, out0, out1).astype(BF16)


def _prompt_attn(q, kt, vt, c, batch, seq):
    t = ATTN_TILE
    nq = seq // t
    pairs = D_ATTN // LANES
    return pl.pallas_call(
        _attn_kernel,
        grid=(batch, pairs, nq),
        in_specs=[
            pl.BlockSpec((t, LANES), lambda b, p, i: (b * nq + i, p)),
            pl.BlockSpec((1, LANES, seq), lambda b, p, i: (b, p, 0)),
            pl.BlockSpec((1, LANES, seq), lambda b, p, i: (b, p, 0)),
            pl.BlockSpec((1, N_HEADS, seq), lambda b, p, i: (b, 0, 0)),
        ],
        out_specs=pl.BlockSpec((t, LANES), lambda b, p, i: (b * nq + i, p)),
        out_shape=jax.ShapeDtypeStruct((batch * seq, D_ATTN), BF16),
        scratch_shapes=[pltpu.VMEM((2, LANES, seq), BF16), pltpu.VMEM((2, LANES, seq), BF16)],
        compiler_params=pltpu.CompilerParams(
            dimension_semantics=("arbitrary", "arbitrary", "arbitrary"),
            vmem_limit_bytes=VMEM_LIMIT),
        name="prompt_attn",
    )(q, kt, vt, c)


def _mix_kernel(alpha, x_ref, a_ref, cb_ref, wg_ref, bg_ref, wpa_ref, wpc_ref, wo_ref,
                g_ref, b_ref, o_ref, m_ref):
    subs = _row_subtiles(x_ref.shape[0])
    for rows in subs:
        xb = x_ref[rows].astype(BF16)
        a = a_ref[rows]
        cb = cb_ref[rows]
        for c in range(0, D_MODEL, MIX_COLS):
            cols = slice(c, c + MIX_COLS)
            gcols = slice(D_MODEL + c, D_MODEL + c + MIX_COLS)
            ga = jax.nn.sigmoid(_dot(xb, wg_ref[:, cols]) + bg_ref[:, cols])
            gc = jax.nn.sigmoid(_dot(xb, wg_ref[:, gcols]) + bg_ref[:, gcols])
            m = ga * _dot(a, wpa_ref[:, cols]) + gc * _dot(cb, wpc_ref[:, cols])
            m_ref[rows, cols] = m.astype(BF16)
    for rows in subs:
        tm = _dot(m_ref[rows], wo_ref[...])
        o_ref[rows] = _layer_norm(alpha * x_ref[rows] + tm, g_ref[...], b_ref[...])


def _mix(layer, x, a, cb, wg, bg, wpa, wpc, wo, g, b, alpha, rows):
    n = x.shape[0]
    row_map = lambda i: (i, 0)
    return pl.pallas_call(
        functools.partial(_mix_kernel, alpha),
        grid=(n // rows,),
        in_specs=[
            pl.BlockSpec((rows, D_MODEL), row_map),
            pl.BlockSpec((rows, D_ATTN), row_map),
            pl.BlockSpec((rows, D_CONV), row_map),
            _resident(wg), _resident(bg),
            _resident(wpa, layer), _resident(wpc, layer), _resident(wo, layer),
            _resident(g), _resident(b),
        ],
        out_specs=pl.BlockSpec((rows, D_MODEL), row_map),
        out_shape=jax.ShapeDtypeStruct((n, D_MODEL), F32),
        scratch_shapes=[pltpu.VMEM((rows, D_MODEL), BF16)],
        compiler_params=pltpu.CompilerParams(
            dimension_semantics=("arbitrary",), vmem_limit_bytes=VMEM_LIMIT),
        name="mix",
    )(x, a, cb, wg, bg, wpa, wpc, wo, g, b)


def _ffn_body(alpha, x_ref, wgu_ref, wd_ref, g_ref, b_ref, o_ref, act_ref):
    subs = _row_subtiles(x_ref.shape[0])
    for rows in subs:
        xb = x_ref[rows].astype(BF16)
        for c in range(0, D_FF, FFN_COLS):
            gate = _dot(xb, wgu_ref[:, c:c + FFN_COLS])
            up = _dot(xb, wgu_ref[:, D_FF + c:D_FF + c + FFN_COLS])
            act_ref[rows, c:c + FFN_COLS] = (jax.nn.silu(gate) * up).astype(BF16)
    for rows in subs:
        f = _dot(act_ref[rows], wd_ref[...])
        o_ref[rows] = _layer_norm(alpha * x_ref[rows] + f, g_ref[...], b_ref[...])


def _resident(a, layer=None):
    if layer is None:
        return pl.BlockSpec(a.shape, lambda *_: (0,) * a.ndim, pipeline_mode=pl.Buffered(1))
    return pl.BlockSpec((None,) + a.shape[1:], lambda *_: (layer,) + (0,) * (a.ndim - 1),
                        pipeline_mode=pl.Buffered(1))


def _ffn(layer, x, wgu, wd, g, b, alpha, rows):
    n = x.shape[0]
    return pl.pallas_call(
        functools.partial(_ffn_body, alpha),
        grid=(n // rows,),
        in_specs=[
            pl.BlockSpec((rows, D_MODEL), lambda i: (i, 0)),
            _resident(wgu, layer), _resident(wd, layer), _resident(g), _resident(b),
        ],
        out_specs=pl.BlockSpec((rows, D_MODEL), lambda i: (i, 0)),
        out_shape=jax.ShapeDtypeStruct((n, D_MODEL), F32),
        scratch_shapes=[pltpu.VMEM((rows, D_FF), BF16)],
        compiler_params=pltpu.CompilerParams(
            dimension_semantics=("arbitrary",), vmem_limit_bytes=VMEM_LIMIT),
        name="ffn",
    )(x, wgu, wd, g, b)


def _sample_proj_kernel(x_ref, wn_ref, wkv_ref, wf_ref, bf_ref, cw_ref, h0_ref, h1_ref,
                        q_ref, k_ref, v_ref, lf_ref, cb_ref, u_ref):
    xb = x_ref[...].astype(BF16)
    z = _dot(xb, wn_ref[...])
    q_ref[...] = z[:, 0:D_ATTN] * SCALE
    kv = _dot_nt(xb, wkv_ref[...])
    k_ref[...] = kv[:, 0:D_ATTN]
    v_ref[...] = kv[:, D_ATTN:2 * D_ATTN]
    lf_ref[...] = jax.nn.log_sigmoid(_dot_nt(wf_ref[...], xb) + bf_ref[...])
    h = z[:, D_ATTN:D_ATTN + D_CONV]
    gb = z[:, D_ATTN + D_CONV:D_ATTN + 2 * D_CONV]
    gc = z[:, D_ATTN + 2 * D_CONV:D_ATTN + 3 * D_CONV]
    u = gc * h
    cw = cw_ref[...]
    cv = cw[0:1] * h0_ref[...] + cw[1:2] * h1_ref[...] + cw[2:3] * u
    cb_ref[...] = (gb * cv).astype(BF16)
    u_ref[...] = u


def _sample_proj(x, wn, wkvt, wft, bf, cw, h0, h1):
    n = x.shape[0]
    args = (x, wn, wkvt, wft, bf, cw, h0, h1)
    full = lambda a: pl.BlockSpec(a.shape, lambda i: (0,) * a.ndim)
    out_shape = [
        jax.ShapeDtypeStruct((n, D_ATTN), F32),
        jax.ShapeDtypeStruct((n, D_ATTN), F32),
        jax.ShapeDtypeStruct((n, D_ATTN), F32),
        jax.ShapeDtypeStruct((N_HEADS, n), F32),
        jax.ShapeDtypeStruct((n, D_CONV), BF16),
        jax.ShapeDtypeStruct((n, D_CONV), F32),
    ]
    return pl.pallas_call(
        _sample_proj_kernel,
        grid=(1,),
        in_specs=[full(a) for a in args],
        out_specs=[full(s) for s in out_shape],
        out_shape=out_shape,
        compiler_params=pltpu.CompilerParams(
            dimension_semantics=("arbitrary",), vmem_limit_bytes=VMEM_LIMIT),
        name="sample_proj",
    )(*args)


def _head_rows(h):
    return slice(h * HEAD_DIM, (h + 1) * HEAD_DIM)


def _paged_logits(h, qb_ref, k_refs, s_ref):
    qh = qb_ref[_head_rows(h), :]
    for j, k in enumerate(k_refs):
        prod = k[_head_rows(h), :] * qh
        s_ref[h:h + 1, j * LANES:(j + 1) * LANES] = jnp.sum(prod, axis=0, keepdims=True)


def _paged_softmax(s_ref, p_ref, a_ref, lf_refs, tri_ref, m_ref, l_ref, carry_ref):
    pages = len(lf_refs)
    m = m_ref[...]
    carry = carry_ref[...]
    lf = jnp.concatenate([r[...] for r in lf_refs], axis=0)
    n = pages * N_HEADS
    r = _dot(jnp.concatenate(_split3(lf), axis=0), tri_ref[...])
    r = r[0:n] + r[n:2 * n] + r[2 * n:3 * n]
    cs = r[:, 0:LANES]
    tot = r[:, LANES:2 * LANES]
    c_pages = []
    for j in range(pages):
        c_pages.append(cs[j * N_HEADS:(j + 1) * N_HEADS] + carry)
        carry = carry + tot[j * N_HEADS:(j + 1) * N_HEADS]
    s = s_ref[...] - jnp.concatenate(c_pages, axis=1)
    m_new = jnp.maximum(m, jnp.max(s, axis=-1, keepdims=True))
    alpha = jnp.exp(m - m_new)
    p = jnp.exp(s - m_new)
    p_ref[...] = p
    a_ref[...] = alpha
    m_ref[...] = m_new
    l_ref[...] = alpha * l_ref[...] + jnp.sum(p, axis=-1, keepdims=True)
    carry_ref[...] = carry


def _paged_values(h, p_ref, a_ref, v_refs, acc_ref):
    acc = acc_ref[_head_rows(h), :] * a_ref[h:h + 1, :]
    for j, v in enumerate(v_refs):
        acc = acc + v[_head_rows(h), :] * p_ref[h:h + 1, j * LANES:(j + 1) * LANES]
    acc_ref[_head_rows(h), :] = acc


def _paged_finish(qc_ref, knc_ref, vnc_ref, lfn_ref, m_ref, l_ref, acc_ref, carry_ref, o_ref):
    m = m_ref[...]
    qk = qc_ref[0] * knc_ref[0]
    s_new = jnp.concatenate([jnp.sum(qk[_head_rows(h)], axis=0, keepdims=True)
                             for h in range(N_HEADS)], axis=0)
    s_new = s_new - (carry_ref[:, 0:1] + lfn_ref[0])
    m_fin = jnp.maximum(m, s_new)
    alpha = jnp.exp(m - m_fin)
    p_new = jnp.exp(s_new - m_fin)
    l_fin = alpha * l_ref[...] + p_new
    for h in range(N_HEADS):
        rows = _head_rows(h)
        num = (alpha[h:h + 1, :] * jnp.sum(acc_ref[rows, :], axis=-1, keepdims=True)
               + p_new[h:h + 1, :] * vnc_ref[0, rows, :])
        o_ref[0, rows, :] = num / l_fin[h:h + 1, :]


def _page_copies(layer, pt_ref, seq, quarter, slot, caches, bufs, sems):
    pages = bufs[0].shape[1]
    copies = []
    for j in range(pages):
        page = pt_ref[seq, quarter * pages + j]
        for kind, (cache, buf) in enumerate(zip(caches, bufs)):
            copies.append(pltpu.make_async_copy(cache.at[layer, page], buf.at[slot, j],
                                                sems.at[slot, kind]))
    return copies


def _ffn_paged_kernel(alpha, layer, pt_ref, x_ref, wgu_ref, wd_ref, g_ref, b_ref,
                      q_ref, kn_ref, vn_ref, lfn_ref, tri_ref, ck_ref, cv_ref, clf_ref,
                      y_ref, o_ref, act_ref, part_ref, qb_ref, s_ref, p_ref, a_ref,
                      m_ref, l_ref, acc_ref, carry_ref, kbuf, vbuf, lfbuf, sems):
    seq = pl.program_id(0)
    quarter = pl.program_id(1)
    n_seq = pl.num_programs(0)
    pages = kbuf.shape[1]
    caches, bufs = (ck_ref, cv_ref, clf_ref), (kbuf, vbuf, lfbuf)
    copies = functools.partial(_page_copies, layer, pt_ref, caches=caches, bufs=bufs, sems=sems)
    subs = _row_subtiles(x_ref.shape[0])

    def gate_up(c0, c1):
        def piece(rows, c):
            xb = x_ref[rows].astype(BF16)
            gate = _dot(xb, wgu_ref[:, c:c + FFN_COLS])
            up = _dot(xb, wgu_ref[:, D_FF + c:D_FF + c + FFN_COLS])
            act_ref[rows, c:c + FFN_COLS] = (jax.nn.silu(gate) * up).astype(BF16)
        return [functools.partial(piece, rows, c)
                for rows in subs for c in range(c0, c1, FFN_COLS)]

    def down(k0, k1, first=False, last=False):
        def piece(rows):
            part = _dot(act_ref[rows, k0:k1], wd_ref[k0:k1, :])
            if first:
                part_ref[rows] = part
            elif not last:
                part_ref[rows] += part
            else:
                y_ref[rows] = _layer_norm(alpha * x_ref[rows] + (part_ref[rows] + part),
                                          g_ref[...], b_ref[...])
        return [functools.partial(piece, rows) for rows in subs]

    c2, c4 = 4 * FFN_COLS, 8 * FFN_COLS
    ffn_work = [
        gate_up(0, c2),
        gate_up(c2, c4),
        gate_up(c4, D_FF) + down(0, c2, first=True),
        down(c2, c4) + down(c4, D_FF, last=True),
    ]
    last = len(ffn_work) - 1
    for k, pieces in enumerate(ffn_work):
        @pl.when(quarter == k)
        def _(k=k, pieces=pieces):
            slot = k % 2
            if k == 0:
                @pl.when(seq == 0)
                def _():
                    for c in copies(seq, k, slot):
                        c.start()
            if k < last:
                for c in copies(seq, k + 1, 1 - slot):
                    c.start()
            else:
                @pl.when(seq + 1 < n_seq)
                def _():
                    for c in copies(seq + 1, 0, 1 - slot):
                        c.start()
            for c in copies(seq, k, slot):
                c.wait()
            k_refs = [kbuf.at[slot, j] for j in range(pages)]
            v_refs = [vbuf.at[slot, j] for j in range(pages)]
            lf_refs = [lfbuf.at[slot, j] for j in range(pages)]
            attn_work = (
                [functools.partial(_paged_logits, h, qb_ref, k_refs, s_ref)
                 for h in range(N_HEADS)]
                + [functools.partial(_paged_softmax, s_ref, p_ref, a_ref, lf_refs, tri_ref,
                                     m_ref, l_ref, carry_ref)]
                + [functools.partial(_paged_values, h, p_ref, a_ref, v_refs, acc_ref)
                   for h in range(N_HEADS)])
            if k == 0:
                m_ref[...] = jnp.full_like(m_ref, -jnp.inf)
                l_ref[...] = jnp.zeros_like(l_ref)
                acc_ref[...] = jnp.zeros_like(acc_ref)
                carry_ref[...] = jnp.zeros_like(carry_ref)
                qb_ref[...] = jnp.broadcast_to(q_ref[0], qb_ref.shape)
            for piece in attn_work + pieces:
                piece()
            if k == last:
                _paged_finish(q_ref, kn_ref, vn_ref, lfn_ref, m_ref, l_ref, acc_ref, carry_ref,
                              o_ref)


def _ffn_paged(layer, page_table, x, wgu, wd, g, b, alpha, q, kn, vn, lfn, tri,
               cache_kt, cache_vt, cache_lft):
    n, n_pages = page_table.shape
    quarters = 4
    assert n_pages % quarters == 0 and x.shape[0] % n == 0
    pages = n_pages // quarters
    rows = x.shape[0] // n
    assert rows % SUBLANES == 0
    x_spec = pl.BlockSpec((rows, D_MODEL), lambda s, k, pt: (s, 0))
    row = pl.BlockSpec((1, D_ATTN, 1), lambda s, k, pt: (s, 0, 0))

    hbm = pl.BlockSpec(memory_space=pl.ANY)
    in_specs = [x_spec, _resident(wgu, layer), _resident(wd, layer), _resident(g), _resident(b),
                row, row, row,
                pl.BlockSpec((1, N_HEADS, 1), lambda s, k, pt: (s, 0, 0)),
                _resident(tri), hbm, hbm, hbm]
    grid_spec = pltpu.PrefetchScalarGridSpec(
        num_scalar_prefetch=1,
        grid=(n, quarters),
        in_specs=in_specs,
        out_specs=[x_spec, row],
        scratch_shapes=[
            pltpu.VMEM((rows, D_FF), BF16),
            pltpu.VMEM((rows, D_MODEL), F32),
            pltpu.VMEM((D_ATTN, LANES), F32),
            pltpu.VMEM((N_HEADS, pages * PAGE_SIZE), F32),
            pltpu.VMEM((N_HEADS, pages * PAGE_SIZE), F32),
            pltpu.VMEM((N_HEADS, 1), F32),
            pltpu.VMEM((N_HEADS, 1), F32),
            pltpu.VMEM((N_HEADS, 1), F32),
            pltpu.VMEM((D_ATTN, LANES), F32),
            pltpu.VMEM((N_HEADS, LANES), F32),
            pltpu.VMEM((2, pages, D_ATTN, PAGE_SIZE), F32),
            pltpu.VMEM((2, pages, D_ATTN, PAGE_SIZE), F32),
            pltpu.VMEM((2, pages, N_HEADS, PAGE_SIZE), F32),
            pltpu.SemaphoreType.DMA((2, 3)),
        ],
    )
    return pl.pallas_call(
        functools.partial(_ffn_paged_kernel, alpha, layer),
        grid_spec=grid_spec,
        out_shape=[jax.ShapeDtypeStruct(x.shape, F32),
                   jax.ShapeDtypeStruct((n, D_ATTN, 1), F32)],
        compiler_params=pltpu.CompilerParams(
            dimension_semantics=("arbitrary", "arbitrary"), vmem_limit_bytes=VMEM_LIMIT),
        name="ffn_paged",
    )(page_table, x, wgu, wd, g, b, q, kn, vn, lfn, tri, cache_kt, cache_vt, cache_lft)


def kernel(x_prompt, x_sample, cache_k, cache_v, cache_logf, state_conv, page_table, w_in, b_f, b_gate, conv_w, w_attn_proj, w_conv_proj, w_out, ln1_g, ln1_b, w_gate_up, w_down, ln2_g, ln2_b):
    batch, seq, _ = x_prompt.shape
    n_dec = x_sample.shape[0]
    depth = w_in.shape[0]
    n_pool = cache_k.shape[1]
    alpha = (2 * depth) ** 0.25

    cache_kt = jnp.transpose(cache_k, (0, 1, 3, 4, 2)).reshape(depth, n_pool, D_ATTN, PAGE_SIZE)
    cache_vt = jnp.transpose(cache_v, (0, 1, 3, 4, 2)).reshape(depth, n_pool, D_ATTN, PAGE_SIZE)
    cache_lft = jnp.transpose(cache_logf, (0, 1, 3, 2))
    tri = jnp.triu(jnp.ones((LANES, LANES), F32)).astype(BF16)
    tri_ones = jnp.concatenate([tri, jnp.ones((LANES, LANES), BF16)], axis=1)

    xp = x_prompt.reshape(batch * seq, D_MODEL)
    xs = x_sample.reshape(n_dec, D_MODEL)
    kp, vp, lp, cp, ksm, vsm, lsm, csm = [], [], [], [], [], [], [], []
    wpa, wpc, wo = w_attn_proj.astype(BF16), w_conv_proj.astype(BF16), w_out.astype(BF16)
    wgu, wd = w_gate_up.astype(BF16), w_down.astype(BF16)
    for l in range(depth):
        wt = jnp.transpose(w_in[l])
        wn = jnp.concatenate([w_in[l][:, OFF_Q:OFF_K], w_in[l][:, OFF_H:OFF_G]], axis=1).astype(BF16)
        wkvt = wt[OFF_K:OFF_F].astype(BF16)
        wft = wt[OFF_F:OFF_H].astype(BF16)
        wg = w_in[l][:, OFF_G:].astype(BF16)
        bf = b_f[l].reshape(N_HEADS, 1)
        bg = b_gate[l].reshape(1, 2 * D_MODEL)
        cw = conv_w[l]
        g1, b1 = ln1_g[l].reshape(1, D_MODEL), ln1_b[l].reshape(1, D_MODEL)
        g2, b2 = ln2_g[l].reshape(1, D_MODEL), ln2_b[l].reshape(1, D_MODEL)

        q, kt, vt, lf, c, cb, st = _prompt_proj(xp, wn, wkvt, wft, bf, cw, tri, batch, seq)
        a = _prompt_attn(q, kt, vt, c, batch, seq)
        x1 = _mix(l, xp, a, cb, wg, bg, wpa, wpc, wo, g1, b1, alpha, MIX_ROWS)
        kp.append(kt); vp.append(vt); lp.append(lf); cp.append(st)
        h0, h1 = state_conv[l, :, 0, :], state_conv[l, :, 1, :]
        qs, ks, vs, lfs, cbs, us = _sample_proj(xs, wn, wkvt, wft, bf, cw, h0, h1)

        xp, a_s = _ffn_paged(l, page_table, x1, wgu, wd, g2, b2, alpha,
                             qs.reshape(n_dec, D_ATTN, 1), ks.reshape(n_dec, D_ATTN, 1),
                             vs.reshape(n_dec, D_ATTN, 1),
                             jnp.transpose(lfs).reshape(n_dec, N_HEADS, 1),
                             tri_ones, cache_kt, cache_vt, cache_lft)

        x1s = _mix(l, xs, a_s.reshape(n_dec, D_ATTN).astype(BF16), cbs, wg, bg, wpa, wpc, wo,
                   g1, b1, alpha, n_dec)
        xs = _ffn(l, x1s, wgu, wd, g2, b2, alpha, n_dec)
        ksm.append(ks); vsm.append(vs); lsm.append(lfs); csm.append(jnp.stack([h1, us], axis=1))

    def heads_last(t):
        return jnp.transpose(t.reshape(depth, batch, N_HEADS, HEAD_DIM, seq), (0, 1, 4, 2, 3))

    y_prompt = xp.reshape(batch, seq, D_MODEL)
    y_sample = xs.reshape(n_dec, 1, D_MODEL)
    k_prompt = heads_last(jnp.stack(kp))
    v_prompt = heads_last(jnp.stack(vp))
    logf_prompt = jnp.transpose(jnp.stack(lp), (0, 1, 3, 2))
    conv_prompt = jnp.stack(cp)
    k_sample = jnp.stack(ksm).reshape(depth, n_dec, 1, N_HEADS, HEAD_DIM)
    v_sample = jnp.stack(vsm).reshape(depth, n_dec, 1, N_HEADS, HEAD_DIM)
    logf_sample = jnp.transpose(jnp.stack(lsm), (0, 2, 1)).reshape(depth, n_dec, 1, N_HEADS)
    conv_sample = jnp.stack(csm)
    return (y_prompt, y_sample, k_prompt, v_prompt, logf_prompt, conv_prompt,
            k_sample, v_sample, logf_sample, conv_sample)
```

```python
import functools

import jax
import jax.numpy as jnp
from jax import lax
from jax.experimental import pallas as pl
from jax.experimental.pallas import tpu as pltpu

D_MODEL = 1024
N_HEADS = 8
HEAD_DIM = 64
D_ATTN = N_HEADS * HEAD_DIM
D_CONV = 512
CONV_W = 3
D_FF = 2816
PAGE_SIZE = 128
LN_EPS = 1e-5
SCALE = HEAD_DIM ** -0.5

OFF_Q = 0
OFF_K = OFF_Q + D_ATTN
OFF_V = OFF_K + D_ATTN
OFF_F = OFF_V + D_ATTN
OFF_H = OFF_F + N_HEADS
OFF_B = OFF_H + D_CONV
OFF_C = OFF_B + D_CONV
OFF_G = OFF_C + D_CONV

LANES = 128
SUBLANES = 8
VMEM_LIMIT = 56 * 1024 * 1024

PROJ_ROWS = 1024
PROJ_COLS = 256
ATTN_TILE = 1024
MIX_ROWS = 1024
MIX_COLS = 256
SUB_ROWS = 256
FFN_COLS = 256
PAGE_SKEW_ROWS = SUBLANES

BF16 = jnp.bfloat16
F32 = jnp.float32

_NT = (((1,), (1,)), ((), ()))


def _dot(a, b):
    return jnp.dot(a, b, preferred_element_type=F32)


def _dot_nt(a, b):
    return lax.dot_general(a, b, _NT, preferred_element_type=F32)


def _split3(x):
    hi = x.astype(BF16)
    r1 = x - hi.astype(F32)
    mid = r1.astype(BF16)
    lo = (r1 - mid.astype(F32)).astype(BF16)
    return hi, mid, lo


def _lane_cumsum(x, tri):
    rows = x.shape[0]
    parts = jnp.concatenate(_split3(x), axis=0)
    cs = _dot(parts, tri)
    return cs[:rows] + cs[rows:2 * rows] + cs[2 * rows:]


def _row_subtiles(rows):
    sub = min(rows, SUB_ROWS)
    return [slice(r, r + sub) for r in range(0, rows, sub)]


def _layer_norm(y, g, b):
    mu = jnp.mean(y, axis=-1, keepdims=True)
    d = y - mu
    var = jnp.mean(d * d, axis=-1, keepdims=True)
    return d * lax.rsqrt(var + LN_EPS) * g + b


def _proj_kernel(x_ref, wn_ref, wkv_ref, wf_ref, bf_ref, cw_ref, tri_ref,
                 q_ref, kt_ref, vt_ref, lf_ref, c_ref, cb_ref, st_ref,
                 hist_ref, carry_ref):
    si = pl.program_id(1)
    rows = x_ref.shape[0]

    @pl.when(si == 0)
    def _():
        hist_ref[...] = jnp.zeros_like(hist_ref)
        carry_ref[...] = jnp.zeros_like(carry_ref)

    xb = x_ref[...].astype(BF16)
    pc = PROJ_COLS
    for c in range(0, D_ATTN, pc):
        q_ref[:, c:c + pc] = (_dot(xb, wn_ref[:, c:c + pc]) * SCALE).astype(BF16)
    for r in range(0, D_ATTN, pc):
        kt_ref[0, r:r + pc] = _dot_nt(wkv_ref[r:r + pc], xb)
        vt_ref[0, r:r + pc] = _dot_nt(wkv_ref[D_ATTN + r:D_ATTN + r + pc], xb)

    lf = jax.nn.log_sigmoid(_dot_nt(wf_ref[...], xb) + bf_ref[...])
    lf_ref[0] = lf
    carry = carry_ref[...]
    for j in range(rows // LANES):
        cs = _lane_cumsum(lf[:, j * LANES:(j + 1) * LANES], tri_ref[...]) + carry
        c_ref[0, :, j * LANES:(j + 1) * LANES] = cs
        carry = cs[:, LANES - 1:LANES]
    carry_ref[...] = carry

    row = lax.broadcasted_iota(jnp.int32, (SUBLANES, pc), 0)
    for c in range(0, D_CONV, pc):
        h = _dot(xb, wn_ref[:, D_ATTN + c:D_ATTN + c + pc])
        gb = _dot(xb, wn_ref[:, D_ATTN + D_CONV + c:D_ATTN + D_CONV + c + pc])
        gc = _dot(xb, wn_ref[:, D_ATTN + 2 * D_CONV + c:D_ATTN + 2 * D_CONV + c + pc])
        u = gc * h
        hist = hist_ref[:, c:c + pc]
        r1 = pltpu.roll(u, 1, 0)
        r2 = pltpu.roll(u, 2, 0)
        h1 = pltpu.roll(hist, 1, 0)
        h2 = pltpu.roll(hist, 2, 0)
        u1 = jnp.concatenate([jnp.where(row < 1, h1, r1[0:SUBLANES]), r1[SUBLANES:]], axis=0)
        u2 = jnp.concatenate([jnp.where(row < 2, h2, r2[0:SUBLANES]), r2[SUBLANES:]], axis=0)
        cw = cw_ref[:, c:c + pc]
        cv = cw[0:1] * u2 + cw[1:2] * u1 + cw[2:3] * u
        cb_ref[:, c:c + pc] = (gb * cv).astype(BF16)
        hist_ref[:, c:c + pc] = u[rows - SUBLANES:rows]

    @pl.when(si == pl.num_programs(1) - 1)
    def _():
        st_ref[0] = hist_ref[SUBLANES - (CONV_W - 1):SUBLANES, :]


def _prompt_proj(x, wn, wkvt, wft, bf, cw, tri, batch, seq):
    rows = PROJ_ROWS
    ns = seq // rows
    row_map = lambda b, s: (b * ns + s, 0)
    const = lambda b, s: (0, 0)
    seq_map = lambda b, s: (b, 0, s)
    return pl.pallas_call(
        _proj_kernel,
        grid=(batch, ns),
        in_specs=[
            pl.BlockSpec((rows, D_MODEL), row_map),
            pl.BlockSpec(wn.shape, const),
            pl.BlockSpec(wkvt.shape, const),
            pl.BlockSpec(wft.shape, const),
            pl.BlockSpec(bf.shape, const),
            pl.BlockSpec(cw.shape, const),
            pl.BlockSpec(tri.shape, const),
        ],
        out_specs=[
            pl.BlockSpec((rows, D_ATTN), row_map),
            pl.BlockSpec((1, D_ATTN, rows), seq_map),
            pl.BlockSpec((1, D_ATTN, rows), seq_map),
            pl.BlockSpec((1, N_HEADS, rows), seq_map),
            pl.BlockSpec((1, N_HEADS, rows), seq_map),
            pl.BlockSpec((rows, D_CONV), row_map),
            pl.BlockSpec((1, CONV_W - 1, D_CONV), lambda b, s: (b, 0, 0)),
        ],
        out_shape=[
            jax.ShapeDtypeStruct((batch * seq, D_ATTN), BF16),
            jax.ShapeDtypeStruct((batch, D_ATTN, seq), F32),
            jax.ShapeDtypeStruct((batch, D_ATTN, seq), F32),
            jax.ShapeDtypeStruct((batch, N_HEADS, seq), F32),
            jax.ShapeDtypeStruct((batch, N_HEADS, seq), F32),
            jax.ShapeDtypeStruct((batch * seq, D_CONV), BF16),
            jax.ShapeDtypeStruct((batch, CONV_W - 1, D_CONV), F32),
        ],
        scratch_shapes=[
            pltpu.VMEM((SUBLANES, D_CONV), F32),
            pltpu.VMEM((N_HEADS, 1), F32),
        ],
        compiler_params=pltpu.CompilerParams(
            dimension_semantics=("arbitrary", "arbitrary"), vmem_limit_bytes=VMEM_LIMIT),
        name="prompt_proj",
    )(x, wn, wkvt, wft, bf, cw, tri)


def _attn_kernel(q_ref, kt_ref, vt_ref, c_ref, o_ref, kb_ref, vb_ref):
    pair = pl.program_id(1)
    qi = pl.program_id(2)
    tq = q_ref.shape[0]
    tk = tq
    seq = kt_ref.shape[2]

    @pl.when(qi == 0)
    def _():
        kt = kt_ref[0]
        vt = vt_ref[0]
        rowi = lax.broadcasted_iota(jnp.int32, (HEAD_DIM, seq), 0)
        ones_row = jnp.where(rowi == 0, 1.0, 0.0).astype(BF16)
        extra = []
        for hh in range(2):
            hi, mid, lo = _split3(-c_ref[0, pl.ds(2 * pair + hh, 1), :])
            e = jnp.where(rowi == 0, hi.astype(F32),
                          jnp.where(rowi == 1, mid.astype(F32),
                                    jnp.where(rowi == 2, lo.astype(F32), 0.0)))
            extra.append(e.astype(BF16))
        kb_ref[0, 0:HEAD_DIM] = kt[0:HEAD_DIM].astype(BF16)
        kb_ref[0, HEAD_DIM:] = extra[0]
        kb_ref[1, 0:HEAD_DIM] = extra[1]
        kb_ref[1, HEAD_DIM:] = kt[HEAD_DIM:].astype(BF16)
        vb_ref[0, 0:HEAD_DIM] = vt[0:HEAD_DIM].astype(BF16)
        vb_ref[0, HEAD_DIM:] = ones_row
        vb_ref[1, 0:HEAD_DIM] = ones_row
        vb_ref[1, HEAD_DIM:] = vt[HEAD_DIM:].astype(BF16)

    q2 = q_ref[...]
    lane = lax.broadcasted_iota(jnp.int32, (1, LANES), 1)
    causal = (lax.broadcasted_iota(jnp.int32, (tq, tk), 1)
              <= lax.broadcasted_iota(jnp.int32, (tq, tk), 0))

    qa = []
    for hh in range(2):
        base = (1 - hh) * HEAD_DIM
        ones3 = jnp.where((lane >= base) & (lane < base + 3), 1.0, 0.0).astype(BF16)
        qa.append(jnp.where((lane // HEAD_DIM) == hh, q2, ones3))

    def step(j, carry, masked):
        col = pl.multiple_of(j * tk, tk)
        if masked:
            half = tq // 2
            chains = [(hh, slice(r, r + half), r + half) for hh in range(2) for r in (0, half)]
        else:
            chains = [(hh, slice(0, tq), tk) for hh in range(2)]

        def logits(hh, rows, ncols):
            return _dot(qa[hh][rows], kb_ref[hh, :, pl.ds(col, ncols)])

        ms, accs = ([], []), ([], [])
        s_next = logits(*chains[0])
        for n, (hh, rows, ncols) in enumerate(chains):
            s = s_next
            if n + 1 < len(chains):
                s_next = logits(*chains[n + 1])
            m, acc = carry[hh]
            if masked:
                s = jnp.where(causal[rows, 0:ncols], s, -jnp.inf)
            m_new = jnp.maximum(m[rows], jnp.max(s, axis=-1, keepdims=True))
            alpha = jnp.exp(m[rows] - m_new)
            p = jnp.exp(s - m_new).astype(BF16)
            pv = _dot_nt(p, vb_ref[hh, :, pl.ds(col, ncols)])
            ms[hh].append(m_new)
            accs[hh].append(alpha * acc[rows] + pv)
        return tuple((jnp.concatenate(ms[hh], axis=0), jnp.concatenate(accs[hh], axis=0))
                     for hh in range(2))

    init = tuple((jnp.full((tq, 1), -jnp.inf, F32), jnp.zeros((tq, LANES), F32)) for _ in range(2))
    carry = lax.fori_loop(0, qi, functools.partial(step, masked=False), init)
    (_, acc0), (_, acc1) = step(qi, carry, True)
    out0 = acc0 / acc0[:, HEAD_DIM:HEAD_DIM + 1]
    out1 = acc1 / acc1[:, 0:1]
    o_ref[...] = jnp.where((lane // HEAD_DIM) == 0, out0, out1).astype(BF16)


def _prompt_attn(q, kt, vt, c, batch, seq):
    t = ATTN_TILE
    nq = seq // t
    pairs = D_ATTN // LANES
    return pl.pallas_call(
        _attn_kernel,
        grid=(batch, pairs, nq),
        in_specs=[
            pl.BlockSpec((t, LANES), lambda b, p, i: (b * nq + i, p)),
            pl.BlockSpec((1, LANES, seq), lambda b, p, i: (b, p, 0)),
            pl.BlockSpec((1, LANES, seq), lambda b, p, i: (b, p, 0)),
            pl.BlockSpec((1, N_HEADS, seq), lambda b, p, i: (b, 0, 0)),
        ],
        out_specs=pl.BlockSpec((t, LANES), lambda b, p, i: (b * nq + i, p)),
        out_shape=jax.ShapeDtypeStruct((batch * seq, D_ATTN), BF16),
        scratch_shapes=[pltpu.VMEM((2, LANES, seq), BF16), pltpu.VMEM((2, LANES, seq), BF16)],
        compiler_params=pltpu.CompilerParams(
            dimension_semantics=("arbitrary", "arbitrary", "arbitrary"),
            vmem_limit_bytes=VMEM_LIMIT),
        name="prompt_attn",
    )(q, kt, vt, c)


def _mix_kernel(alpha, x_ref, a_ref, cb_ref, wg_ref, bg_ref, wpa_ref, wpc_ref, wo_ref,
                g_ref, b_ref, o_ref, m_ref):
    subs = _row_subtiles(x_ref.shape[0])
    for rows in subs:
        xb = x_ref[rows].astype(BF16)
        a = a_ref[rows]
        cb = cb_ref[rows]
        for c in range(0, D_MODEL, MIX_COLS):
            cols = slice(c, c + MIX_COLS)
            gcols = slice(D_MODEL + c, D_MODEL + c + MIX_COLS)
            ga = jax.nn.sigmoid(_dot(xb, wg_ref[:, cols]) + bg_ref[:, cols])
            gc = jax.nn.sigmoid(_dot(xb, wg_ref[:, gcols]) + bg_ref[:, gcols])
            m = ga * _dot(a, wpa_ref[:, cols]) + gc * _dot(cb, wpc_ref[:, cols])
            m_ref[rows, cols] = m.astype(BF16)
    for rows in subs:
        tm = _dot(m_ref[rows], wo_ref[...])
        o_ref[rows] = _layer_norm(alpha * x_ref[rows] + tm, g_ref[...], b_ref[...])


def _mix(layer, x, a, cb, wg, bg, wpa, wpc, wo, g, b, alpha, rows):
    n = x.shape[0]
    row_map = lambda i: (i, 0)
    return pl.pallas_call(
        functools.partial(_mix_kernel, alpha),
        grid=(n // rows,),
        in_specs=[
            pl.BlockSpec((rows, D_MODEL), row_map),
            pl.BlockSpec((rows, D_ATTN), row_map),
            pl.BlockSpec((rows, D_CONV), row_map),
            _resident(wg), _resident(bg),
            _resident(wpa, layer), _resident(wpc, layer), _resident(wo, layer),
            _resident(g), _resident(b),
        ],
        out_specs=pl.BlockSpec((rows, D_MODEL), row_map),
        out_shape=jax.ShapeDtypeStruct((n, D_MODEL), F32),
        scratch_shapes=[pltpu.VMEM((rows, D_MODEL), BF16)],
        compiler_params=pltpu.CompilerParams(
            dimension_semantics=("arbitrary",), vmem_limit_bytes=VMEM_LIMIT),
        name="mix",
    )(x, a, cb, wg, bg, wpa, wpc, wo, g, b)


def _ffn_body(alpha, x_ref, wgu_ref, wd_ref, g_ref, b_ref, o_ref, act_ref):
    subs = _row_subtiles(x_ref.shape[0])
    for rows in subs:
        xb = x_ref[rows].astype(BF16)
        for c in range(0, D_FF, FFN_COLS):
            gate = _dot(xb, wgu_ref[:, c:c + FFN_COLS])
            up = _dot(xb, wgu_ref[:, D_FF + c:D_FF + c + FFN_COLS])
            act_ref[rows, c:c + FFN_COLS] = (jax.nn.silu(gate) * up).astype(BF16)
    for rows in subs:
        f = _dot(act_ref[rows], wd_ref[...])
        o_ref[rows] = _layer_norm(alpha * x_ref[rows] + f, g_ref[...], b_ref[...])


def _resident(a, layer=None):
    if layer is None:
        return pl.BlockSpec(a.shape, lambda *_: (0,) * a.ndim, pipeline_mode=pl.Buffered(1))
    return pl.BlockSpec((None,) + a.shape[1:], lambda *_: (layer,) + (0,) * (a.ndim - 1),
                        pipeline_mode=pl.Buffered(1))


def _ffn(layer, x, wgu, wd, g, b, alpha, rows):
    n = x.shape[0]
    return pl.pallas_call(
        functools.partial(_ffn_body, alpha),
        grid=(n // rows,),
        in_specs=[
            pl.BlockSpec((rows, D_MODEL), lambda i: (i, 0)),
            _resident(wgu, layer), _resident(wd, layer), _resident(g), _resident(b),
        ],
        out_specs=pl.BlockSpec((rows, D_MODEL), lambda i: (i, 0)),
        out_shape=jax.ShapeDtypeStruct((n, D_MODEL), F32),
        scratch_shapes=[pltpu.VMEM((rows, D_FF), BF16)],
        compiler_params=pltpu.CompilerParams(
            dimension_semantics=("arbitrary",), vmem_limit_bytes=VMEM_LIMIT),
        name="ffn",
    )(x, wgu, wd, g, b)


def _sample_proj_kernel(x_ref, wn_ref, wkv_ref, wf_ref, bf_ref, cw_ref, h0_ref, h1_ref,
                        q_ref, k_ref, v_ref, lf_ref, cb_ref, u_ref):
    xb = x_ref[...].astype(BF16)
    z = _dot(xb, wn_ref[...])
    q_ref[...] = z[:, 0:D_ATTN] * SCALE
    kv = _dot_nt(xb, wkv_ref[...])
    k_ref[...] = kv[:, 0:D_ATTN]
    v_ref[...] = kv[:, D_ATTN:2 * D_ATTN]
    lf_ref[...] = jax.nn.log_sigmoid(_dot_nt(wf_ref[...], xb) + bf_ref[...])
    h = z[:, D_ATTN:D_ATTN + D_CONV]
    gb = z[:, D_ATTN + D_CONV:D_ATTN + 2 * D_CONV]
    gc = z[:, D_ATTN + 2 * D_CONV:D_ATTN + 3 * D_CONV]
    u = gc * h
    cw = cw_ref[...]
    cv = cw[0:1] * h0_ref[...] + cw[1:2] * h1_ref[...] + cw[2:3] * u
    cb_ref[...] = (gb * cv).astype(BF16)
    u_ref[...] = u


def _sample_proj(x, wn, wkvt, wft, bf, cw, h0, h1):
    n = x.shape[0]
    args = (x, wn, wkvt, wft, bf, cw, h0, h1)
    full = lambda a: pl.BlockSpec(a.shape, lambda i: (0,) * a.ndim)
    out_shape = [
        jax.ShapeDtypeStruct((n, D_ATTN), F32),
        jax.ShapeDtypeStruct((n, D_ATTN), F32),
        jax.ShapeDtypeStruct((n, D_ATTN), F32),
        jax.ShapeDtypeStruct((N_HEADS, n), F32),
        jax.ShapeDtypeStruct((n, D_CONV), BF16),
        jax.ShapeDtypeStruct((n, D_CONV), F32),
    ]
    return pl.pallas_call(
        _sample_proj_kernel,
        grid=(1,),
        in_specs=[full(a) for a in args],
        out_specs=[full(s) for s in out_shape],
        out_shape=out_shape,
        compiler_params=pltpu.CompilerParams(
            dimension_semantics=("arbitrary",), vmem_limit_bytes=VMEM_LIMIT),
        name="sample_proj",
    )(*args)


def _head_rows(h):
    return slice(h * HEAD_DIM, (h + 1) * HEAD_DIM)


def _paged_logits(h, qb_ref, k_refs, s_ref):
    qh = qb_ref[_head_rows(h), :]
    for j, k in enumerate(k_refs):
        prod = k[_head_rows(h), :] * qh
        s_ref[h:h + 1, j * LANES:(j + 1) * LANES] = jnp.sum(prod, axis=0, keepdims=True)


def _paged_softmax(s_ref, p_ref, a_ref, lf_refs, tri_ref, m_ref, l_ref, carry_ref):
    pages = len(lf_refs)
    m = m_ref[...]
    carry = carry_ref[...]
    lf = jnp.concatenate([r[...] for r in lf_refs], axis=0)
    n = pages * N_HEADS
    r = _dot(jnp.concatenate(_split3(lf), axis=0), tri_ref[...])
    r = r[0:n] + r[n:2 * n] + r[2 * n:3 * n]
    cs = r[:, 0:LANES]
    tot = r[:, LANES:2 * LANES]
    c_pages = []
    for j in range(pages):
        c_pages.append(cs[j * N_HEADS:(j + 1) * N_HEADS] + carry)
        carry = carry + tot[j * N_HEADS:(j + 1) * N_HEADS]
    s = s_ref[...] - jnp.concatenate(c_pages, axis=1)
    m_new = jnp.maximum(m, jnp.max(s, axis=-1, keepdims=True))
    alpha = jnp.exp(m - m_new)
    p = jnp.exp(s - m_new)
    p_ref[...] = p
    a_ref[...] = alpha
    m_ref[...] = m_new
    l_ref[...] = alpha * l_ref[...] + jnp.sum(p, axis=-1, keepdims=True)
    carry_ref[...] = carry


def _paged_values(h, p_ref, a_ref, v_refs, acc_ref):
    acc = acc_ref[_head_rows(h), :] * a_ref[h:h + 1, :]
    for j, v in enumerate(v_refs):
        acc = acc + v[_head_rows(h), :] * p_ref[h:h + 1, j * LANES:(j + 1) * LANES]
    acc_ref[_head_rows(h), :] = acc


def _paged_finish(qc_ref, knc_ref, vnc_ref, lfn_ref, m_ref, l_ref, acc_ref, carry_ref, o_ref):
    m = m_ref[...]
    qk = qc_ref[0] * knc_ref[0]
    s_new = jnp.concatenate([jnp.sum(qk[_head_rows(h)], axis=0, keepdims=True)
                             for h in range(N_HEADS)], axis=0)
    s_new = s_new - (carry_ref[:, 0:1] + lfn_ref[0])
    m_fin = jnp.maximum(m, s_new)
    alpha = jnp.exp(m - m_fin)
    p_new = jnp.exp(s_new - m_fin)
    l_fin = alpha * l_ref[...] + p_new
    for h in range(N_HEADS):
        rows = _head_rows(h)
        num = (alpha[h:h + 1, :] * jnp.sum(acc_ref[rows, :], axis=-1, keepdims=True)
               + p_new[h:h + 1, :] * vnc_ref[0, rows, :])
        o_ref[0, rows, :] = num / l_fin[h:h + 1, :]


def _page_copies(layer, pt_ref, seq, quarter, slot, caches, bufs, sems):
    pages = bufs[0].shape[1]
    copies = []
    for j in range(pages):
        page = pt_ref[seq, quarter * pages + j]
        for kind, (cache, buf) in enumerate(zip(caches, bufs)):
            copies.append(pltpu.make_async_copy(cache.at[layer, page], _page_view(buf, slot, j),
                                                sems.at[slot, kind]))
    return copies


def _page_view(buf, slot, j):
    return buf.at[slot, j, pl.ds(0, buf.shape[2] - PAGE_SKEW_ROWS)]


def _ffn_paged_kernel(alpha, layer, pt_ref, x_ref, wgu_ref, wd_ref, g_ref, b_ref,
                      q_ref, kn_ref, vn_ref, lfn_ref, tri_ref, ck_ref, cv_ref, clf_ref,
                      y_ref, o_ref, act_ref, part_ref, qb_ref, s_ref, p_ref, a_ref,
                      m_ref, l_ref, acc_ref, carry_ref, kbuf, vbuf, lfbuf, sems):
    seq = pl.program_id(0)
    quarter = pl.program_id(1)
    n_seq = pl.num_programs(0)
    pages = kbuf.shape[1]
    caches, bufs = (ck_ref, cv_ref, clf_ref), (kbuf, vbuf, lfbuf)
    copies = functools.partial(_page_copies, layer, pt_ref, caches=caches, bufs=bufs, sems=sems)
    subs = _row_subtiles(x_ref.shape[0])

    def gate_up(c0, c1):
        def piece(rows, c):
            xb = x_ref[rows].astype(BF16)
            gate = _dot(xb, wgu_ref[:, c:c + FFN_COLS])
            up = _dot(xb, wgu_ref[:, D_FF + c:D_FF + c + FFN_COLS])
            act_ref[rows, c:c + FFN_COLS] = (jax.nn.silu(gate) * up).astype(BF16)
        return [functools.partial(piece, rows, c)
                for rows in subs for c in range(c0, c1, FFN_COLS)]

    def down(k0, k1, first=False, last=False):
        def piece(rows):
            part = _dot(act_ref[rows, k0:k1], wd_ref[k0:k1, :])
            if first:
                part_ref[rows] = part
            elif not last:
                part_ref[rows] += part
            else:
                y_ref[rows] = _layer_norm(alpha * x_ref[rows] + (part_ref[rows] + part),
                                          g_ref[...], b_ref[...])
        return [functools.partial(piece, rows) for rows in subs]

    c2, c4 = 4 * FFN_COLS, 8 * FFN_COLS
    ffn_work = [
        gate_up(0, c2),
        gate_up(c2, c4),
        gate_up(c4, D_FF) + down(0, c2, first=True),
        down(c2, c4) + down(c4, D_FF, last=True),
    ]
    last = len(ffn_work) - 1
    for k, pieces in enumerate(ffn_work):
        @pl.when(quarter == k)
        def _(k=k, pieces=pieces):
            slot = k % 2
            if k == 0:
                @pl.when(seq == 0)
                def _():
                    for c in copies(seq, k, slot):
                        c.start()
            if k < last:
                for c in copies(seq, k + 1, 1 - slot):
                    c.start()
            else:
                @pl.when(seq + 1 < n_seq)
                def _():
                    for c in copies(seq + 1, 0, 1 - slot):
                        c.start()
            for c in copies(seq, k, slot):
                c.wait()
            k_refs = [_page_view(kbuf, slot, j) for j in range(pages)]
            v_refs = [_page_view(vbuf, slot, j) for j in range(pages)]
            lf_refs = [_page_view(lfbuf, slot, j) for j in range(pages)]
            attn_work = (
                [functools.partial(_paged_logits, h, qb_ref, k_refs, s_ref)
                 for h in range(N_HEADS)]
                + [functools.partial(_paged_softmax, s_ref, p_ref, a_ref, lf_refs, tri_ref,
                                     m_ref, l_ref, carry_ref)]
                + [functools.partial(_paged_values, h, p_ref, a_ref, v_refs, acc_ref)
                   for h in range(N_HEADS)])
            if k == 0:
                m_ref[...] = jnp.full_like(m_ref, -jnp.inf)
                l_ref[...] = jnp.zeros_like(l_ref)
                acc_ref[...] = jnp.zeros_like(acc_ref)
                carry_ref[...] = jnp.zeros_like(carry_ref)
                qb_ref[...] = jnp.broadcast_to(q_ref[0], qb_ref.shape)
            for piece in attn_work + pieces:
                piece()
            if k == last:
                _paged_finish(q_ref, kn_ref, vn_ref, lfn_ref, m_ref, l_ref, acc_ref, carry_ref,
                              o_ref)


def _ffn_paged(layer, page_table, x, wgu, wd, g, b, alpha, q, kn, vn, lfn, tri,
               cache_kt, cache_vt, cache_lft):
    n, n_pages = page_table.shape
    quarters = 4
    assert n_pages % quarters == 0 and x.shape[0] % n == 0
    pages = n_pages // quarters
    rows = x.shape[0] // n
    assert rows % SUBLANES == 0
    x_spec = pl.BlockSpec((rows, D_MODEL), lambda s, k, pt: (s, 0))
    row = pl.BlockSpec((1, D_ATTN, 1), lambda s, k, pt: (s, 0, 0))

    hbm = pl.BlockSpec(memory_space=pl.ANY)
    in_specs = [x_spec, _resident(wgu, layer), _resident(wd, layer), _resident(g), _resident(b),
                row, row, row,
                pl.BlockSpec((1, N_HEADS, 1), lambda s, k, pt: (s, 0, 0)),
                _resident(tri), hbm, hbm, hbm]
    grid_spec = pltpu.PrefetchScalarGridSpec(
        num_scalar_prefetch=1,
        grid=(n, quarters),
        in_specs=in_specs,
        out_specs=[x_spec, row],
        scratch_shapes=[
            pltpu.VMEM((rows, D_FF), BF16),
            pltpu.VMEM((rows, D_MODEL), F32),
            pltpu.VMEM((D_ATTN, LANES), F32),
            pltpu.VMEM((N_HEADS, pages * PAGE_SIZE), F32),
            pltpu.VMEM((N_HEADS, pages * PAGE_SIZE), F32),
            pltpu.VMEM((N_HEADS, 1), F32),
            pltpu.VMEM((N_HEADS, 1), F32),
            pltpu.VMEM((N_HEADS, 1), F32),
            pltpu.VMEM((D_ATTN, LANES), F32),
            pltpu.VMEM((N_HEADS, LANES), F32),
            pltpu.VMEM((2, pages, D_ATTN + PAGE_SKEW_ROWS, PAGE_SIZE), F32),
            pltpu.VMEM((2, pages, D_ATTN + PAGE_SKEW_ROWS, PAGE_SIZE), F32),
            pltpu.VMEM((2, pages, N_HEADS + PAGE_SKEW_ROWS, PAGE_SIZE), F32),
            pltpu.SemaphoreType.DMA((2, 3)),
        ],
    )
    return pl.pallas_call(
        functools.partial(_ffn_paged_kernel, alpha, layer),
        grid_spec=grid_spec,
        out_shape=[jax.ShapeDtypeStruct(x.shape, F32),
                   jax.ShapeDtypeStruct((n, D_ATTN, 1), F32)],
        compiler_params=pltpu.CompilerParams(
            dimension_semantics=("arbitrary", "arbitrary"), vmem_limit_bytes=VMEM_LIMIT),
        name="ffn_paged",
    )(page_table, x, wgu, wd, g, b, q, kn, vn, lfn, tri, cache_kt, cache_vt, cache_lft)


def kernel(x_prompt, x_sample, cache_k, cache_v, cache_logf, state_conv, page_table, w_in, b_f, b_gate, conv_w, w_attn_proj, w_conv_proj, w_out, ln1_g, ln1_b, w_gate_up, w_down, ln2_g, ln2_b):
    batch, seq, _ = x_prompt.shape
    n_dec = x_sample.shape[0]
    depth = w_in.shape[0]
    n_pool = cache_k.shape[1]
    alpha = (2 * depth) ** 0.25

    cache_kt = jnp.transpose(cache_k, (0, 1, 3, 4, 2)).reshape(depth, n_pool, D_ATTN, PAGE_SIZE)
    cache_vt = jnp.transpose(cache_v, (0, 1, 3, 4, 2)).reshape(depth, n_pool, D_ATTN, PAGE_SIZE)
    cache_lft = jnp.transpose(cache_logf, (0, 1, 3, 2))
    tri = jnp.triu(jnp.ones((LANES, LANES), F32)).astype(BF16)
    tri_ones = jnp.concatenate([tri, jnp.ones((LANES, LANES), BF16)], axis=1)

    xp = x_prompt.reshape(batch * seq, D_MODEL)
    xs = x_sample.reshape(n_dec, D_MODEL)
    kp, vp, lp, cp, ksm, vsm, lsm, csm = [], [], [], [], [], [], [], []
    wpa, wpc, wo = w_attn_proj.astype(BF16), w_conv_proj.astype(BF16), w_out.astype(BF16)
    wgu, wd = w_gate_up.astype(BF16), w_down.astype(BF16)
    for l in range(depth):
        wt = jnp.transpose(w_in[l])
        wn = jnp.concatenate([w_in[l][:, OFF_Q:OFF_K], w_in[l][:, OFF_H:OFF_G]], axis=1).astype(BF16)
        wkvt = wt[OFF_K:OFF_F].astype(BF16)
        wft = wt[OFF_F:OFF_H].astype(BF16)
        wg = w_in[l][:, OFF_G:].astype(BF16)
        bf = b_f[l].reshape(N_HEADS, 1)
        bg = b_gate[l].reshape(1, 2 * D_MODEL)
        cw = conv_w[l]
        g1, b1 = ln1_g[l].reshape(1, D_MODEL), ln1_b[l].reshape(1, D_MODEL)
        g2, b2 = ln2_g[l].reshape(1, D_MODEL), ln2_b[l].reshape(1, D_MODEL)

        q, kt, vt, lf, c, cb, st = _prompt_proj(xp, wn, wkvt, wft, bf, cw, tri, batch, seq)
        a = _prompt_attn(q, kt, vt, c, batch, seq)
        x1 = _mix(l, xp, a, cb, wg, bg, wpa, wpc, wo, g1, b1, alpha, MIX_ROWS)
        kp.append(kt); vp.append(vt); lp.append(lf); cp.append(st)
        h0, h1 = state_conv[l, :, 0, :], state_conv[l, :, 1, :]
        qs, ks, vs, lfs, cbs, us = _sample_proj(xs, wn, wkvt, wft, bf, cw, h0, h1)

        xp, a_s = _ffn_paged(l, page_table, x1, wgu, wd, g2, b2, alpha,
                             qs.reshape(n_dec, D_ATTN, 1), ks.reshape(n_dec, D_ATTN, 1),
                             vs.reshape(n_dec, D_ATTN, 1),
                             jnp.transpose(lfs).reshape(n_dec, N_HEADS, 1),
                             tri_ones, cache_kt, cache_vt, cache_lft)

        x1s = _mix(l, xs, a_s.reshape(n_dec, D_ATTN).astype(BF16), cbs, wg, bg, wpa, wpc, wo,
                   g1, b1, alpha, n_dec)
        xs = _ffn(l, x1s, wgu, wd, g2, b2, alpha, n_dec)
        ksm.append(ks); vsm.append(vs); lsm.append(lfs); csm.append(jnp.stack([h1, us], axis=1))

    def heads_last(t):
        return jnp.transpose(t.reshape(depth, batch, N_HEADS, HEAD_DIM, seq), (0, 1, 4, 2, 3))

    y_prompt = xp.reshape(batch, seq, D_MODEL)
    y_sample = xs.reshape(n_dec, 1, D_MODEL)
    k_prompt = heads_last(jnp.stack(kp))
    v_prompt = heads_last(jnp.stack(vp))
    logf_prompt = jnp.transpose(jnp.stack(lp), (0, 1, 3, 2))
    conv_prompt = jnp.stack(cp)
    k_sample = jnp.stack(ksm).reshape(depth, n_dec, 1, N_HEADS, HEAD_DIM)
    v_sample = jnp.stack(vsm).reshape(depth, n_dec, 1, N_HEADS, HEAD_DIM)
    logf_sample = jnp.transpose(jnp.stack(lsm), (0, 2, 1)).reshape(depth, n_dec, 1, N_HEADS)
    conv_sample = jnp.stack(csm)
    return (y_prompt, y_sample, k_prompt, v_prompt, logf_prompt, conv_prompt,
            k_sample, v_sample, logf_sample, conv_sample)
```

```python
import functools

import jax
import jax.numpy as jnp
from jax import lax
from jax.experimental import pallas as pl
from jax.experimental.pallas import tpu as pltpu

D_MODEL = 1024
N_HEADS = 8
HEAD_DIM = 64
D_ATTN = N_HEADS * HEAD_DIM
D_CONV = 512
CONV_W = 3
D_FF = 2816
PAGE_SIZE = 128
LN_EPS = 1e-5
SCALE = HEAD_DIM ** -0.5

OFF_Q = 0
OFF_K = OFF_Q + D_ATTN
OFF_V = OFF_K + D_ATTN
OFF_F = OFF_V + D_ATTN
OFF_H = OFF_F + N_HEADS
OFF_B = OFF_H + D_CONV
OFF_C = OFF_B + D_CONV
OFF_G = OFF_C + D_CONV

LANES = 128
SUBLANES = 8
VMEM_LIMIT = 56 * 1024 * 1024

PROJ_ROWS = 1024
PROJ_COLS = 256
ATTN_TILE = 1024
MIX_ROWS = 1024
MIX_COLS = 256
SUB_ROWS = 256
FFN_COLS = 256
PAGE_SKEW_ROWS = SUBLANES

BF16 = jnp.bfloat16
F32 = jnp.float32

_NT = (((1,), (1,)), ((), ()))


def _dot(a, b):
    return jnp.dot(a, b, preferred_element_type=F32)


def _dot_nt(a, b):
    return lax.dot_general(a, b, _NT, preferred_element_type=F32)


def _split3(x):
    hi = x.astype(BF16)
    r1 = x - hi.astype(F32)
    mid = r1.astype(BF16)
    lo = (r1 - mid.astype(F32)).astype(BF16)
    return hi, mid, lo


def _lane_cumsum(x, tri):
    rows = x.shape[0]
    parts = jnp.concatenate(_split3(x), axis=0)
    cs = _dot(parts, tri)
    return cs[:rows] + cs[rows:2 * rows] + cs[2 * rows:]


def _row_subtiles(rows):
    sub = min(rows, SUB_ROWS)
    return [slice(r, r + sub) for r in range(0, rows, sub)]


def _layer_norm(y, g, b):
    mu = jnp.mean(y, axis=-1, keepdims=True)
    d = y - mu
    var = jnp.mean(d * d, axis=-1, keepdims=True)
    return d * lax.rsqrt(var + LN_EPS) * g + b


def _proj_kernel(n_aliased, x_ref, wn_ref, wkv_ref, wf_ref, bf_ref, cw_ref, tri_ref, *refs):
    q_ref, kt_ref, vt_ref, lf_ref, c_ref, cb_ref, st_ref, hist_ref, carry_ref = refs[n_aliased:]
    si = pl.program_id(1)
    rows = x_ref.shape[0]

    @pl.when(si == 0)
    def _():
        hist_ref[...] = jnp.zeros_like(hist_ref)
        carry_ref[...] = jnp.zeros_like(carry_ref)

    xb = x_ref[...].astype(BF16)
    pc = PROJ_COLS
    for c in range(0, D_ATTN, pc):
        q_ref[:, c:c + pc] = (_dot(xb, wn_ref[:, c:c + pc]) * SCALE).astype(BF16)
    for r in range(0, D_ATTN, pc):
        kt = _dot_nt(wkv_ref[r:r + pc], xb)
        vt = _dot_nt(wkv_ref[D_ATTN + r:D_ATTN + r + pc], xb)
        for slot in range(kt_ref.shape[0]):
            kt_ref[slot, 0, r:r + pc] = kt
            vt_ref[slot, 0, r:r + pc] = vt

    lf = jax.nn.log_sigmoid(_dot_nt(wf_ref[...], xb) + bf_ref[...])
    lf_ref[0] = lf
    carry = carry_ref[...]
    for j in range(rows // LANES):
        cs = _lane_cumsum(lf[:, j * LANES:(j + 1) * LANES], tri_ref[...]) + carry
        c_ref[0, :, j * LANES:(j + 1) * LANES] = cs
        carry = cs[:, LANES - 1:LANES]
    carry_ref[...] = carry

    row = lax.broadcasted_iota(jnp.int32, (SUBLANES, pc), 0)
    for c in range(0, D_CONV, pc):
        h = _dot(xb, wn_ref[:, D_ATTN + c:D_ATTN + c + pc])
        gb = _dot(xb, wn_ref[:, D_ATTN + D_CONV + c:D_ATTN + D_CONV + c + pc])
        gc = _dot(xb, wn_ref[:, D_ATTN + 2 * D_CONV + c:D_ATTN + 2 * D_CONV + c + pc])
        u = gc * h
        hist = hist_ref[:, c:c + pc]
        r1 = pltpu.roll(u, 1, 0)
        r2 = pltpu.roll(u, 2, 0)
        h1 = pltpu.roll(hist, 1, 0)
        h2 = pltpu.roll(hist, 2, 0)
        u1 = jnp.concatenate([jnp.where(row < 1, h1, r1[0:SUBLANES]), r1[SUBLANES:]], axis=0)
        u2 = jnp.concatenate([jnp.where(row < 2, h2, r2[0:SUBLANES]), r2[SUBLANES:]], axis=0)
        cw = cw_ref[:, c:c + pc]
        cv = cw[0:1] * u2 + cw[1:2] * u1 + cw[2:3] * u
        cb_ref[:, c:c + pc] = (gb * cv).astype(BF16)
        hist_ref[:, c:c + pc] = u[rows - SUBLANES:rows]

    @pl.when(si == pl.num_programs(1) - 1)
    def _():
        st_ref[0] = hist_ref[SUBLANES - (CONV_W - 1):SUBLANES, :]


def _prompt_proj(layer, depth, x, wn, wkvt, wft, bf, cw, tri, batch, seq, kv_all=()):
    rows = PROJ_ROWS
    ns = seq // rows
    row_map = lambda b, s: (b * ns + s, 0)
    seq_map = lambda b, s: (b, 0, s)
    if kv_all:
        kv_spec = pl.BlockSpec((1, 1, D_ATTN, rows), lambda b, s: (layer, b, 0, s))
    else:
        kv_spec = pl.BlockSpec((depth, 1, D_ATTN, rows), lambda b, s: (0, b, 0, s))
    consts = (wn, wkvt, wft, bf, cw, tri)
    return pl.pallas_call(
        functools.partial(_proj_kernel, len(kv_all)),
        grid=(batch, ns),
        in_specs=([pl.BlockSpec((rows, D_MODEL), row_map)] + [_resident(a) for a in consts]
                  + [pl.BlockSpec(memory_space=pl.ANY) for _ in kv_all]),
        out_specs=[
            pl.BlockSpec((rows, D_ATTN), row_map),
            kv_spec,
            kv_spec,
            pl.BlockSpec((1, N_HEADS, rows), seq_map),
            pl.BlockSpec((1, N_HEADS, rows), seq_map),
            pl.BlockSpec((rows, D_CONV), row_map),
            pl.BlockSpec((1, CONV_W - 1, D_CONV), lambda b, s: (b, 0, 0)),
        ],
        out_shape=[
            jax.ShapeDtypeStruct((batch * seq, D_ATTN), BF16),
            jax.ShapeDtypeStruct((depth, batch, D_ATTN, seq), F32),
            jax.ShapeDtypeStruct((depth, batch, D_ATTN, seq), F32),
            jax.ShapeDtypeStruct((batch, N_HEADS, seq), F32),
            jax.ShapeDtypeStruct((batch, N_HEADS, seq), F32),
            jax.ShapeDtypeStruct((batch * seq, D_CONV), BF16),
            jax.ShapeDtypeStruct((batch, CONV_W - 1, D_CONV), F32),
        ],
        input_output_aliases={1 + len(consts) + i: 1 + i for i in range(len(kv_all))},
        scratch_shapes=[
            pltpu.VMEM((SUBLANES, D_CONV), F32),
            pltpu.VMEM((N_HEADS, 1), F32),
        ],
        compiler_params=pltpu.CompilerParams(
            dimension_semantics=("arbitrary", "arbitrary"), vmem_limit_bytes=VMEM_LIMIT),
        name="prompt_proj",
    )(x, *consts, *kv_all)


def _attn_kernel(q_ref, kt_ref, vt_ref, c_ref, o_ref, kb_ref, vb_ref):
    pair = pl.program_id(1)
    qi = pl.program_id(2)
    tq = q_ref.shape[0]
    tk = tq
    seq = kt_ref.shape[2]

    @pl.when(qi == 0)
    def _():
        kt = kt_ref[0]
        vt = vt_ref[0]
        rowi = lax.broadcasted_iota(jnp.int32, (HEAD_DIM, seq), 0)
        ones_row = jnp.where(rowi == 0, 1.0, 0.0).astype(BF16)
        extra = []
        for hh in range(2):
            hi, mid, lo = _split3(-c_ref[0, pl.ds(2 * pair + hh, 1), :])
            e = jnp.where(rowi == 0, hi.astype(F32),
                          jnp.where(rowi == 1, mid.astype(F32),
                                    jnp.where(rowi == 2, lo.astype(F32), 0.0)))
            extra.append(e.astype(BF16))
        kb_ref[0, 0:HEAD_DIM] = kt[0:HEAD_DIM].astype(BF16)
        kb_ref[0, HEAD_DIM:] = extra[0]
        kb_ref[1, 0:HEAD_DIM] = extra[1]
        kb_ref[1, HEAD_DIM:] = kt[HEAD_DIM:].astype(BF16)
        vb_ref[0, 0:HEAD_DIM] = vt[0:HEAD_DIM].astype(BF16)
        vb_ref[0, HEAD_DIM:] = ones_row
        vb_ref[1, 0:HEAD_DIM] = ones_row
        vb_ref[1, HEAD_DIM:] = vt[HEAD_DIM:].astype(BF16)

    q2 = q_ref[...]
    lane = lax.broadcasted_iota(jnp.int32, (1, LANES), 1)
    causal = (lax.broadcasted_iota(jnp.int32, (tq, tk), 1)
              <= lax.broadcasted_iota(jnp.int32, (tq, tk), 0))

    qa = []
    for hh in range(2):
        base = (1 - hh) * HEAD_DIM
        ones3 = jnp.where((lane >= base) & (lane < base + 3), 1.0, 0.0).astype(BF16)
        qa.append(jnp.where((lane // HEAD_DIM) == hh, q2, ones3))

    def update(s, m, acc, hh, col, ncols):
        m_new = jnp.maximum(m, jnp.max(s, axis=-1, keepdims=True))
        alpha = jnp.exp(m - m_new)
        p = jnp.exp(s - m_new).astype(BF16)
        pv = _dot_nt(p, vb_ref[hh, :, pl.ds(col, ncols)])
        return m_new, alpha * acc + pv

    def full_tiles(js, carry):
        cols = [pl.multiple_of(j * tk, tk) for j in js]
        logits = [[_dot(qa[hh], kb_ref[hh, :, pl.ds(col, tk)]) for hh in range(2)]
                  for col in cols]
        carry = list(carry)
        for col, s_pair in zip(cols, logits):
            for hh in range(2):
                carry[hh] = update(s_pair[hh], *carry[hh], hh, col, tk)
        return tuple(carry)

    def diagonal_tile(carry):
        col = pl.multiple_of(qi * tk, tk)
        half = tq // 2
        chains = [(hh, slice(r, r + half), r + half) for hh in range(2) for r in (0, half)]
        logits = [_dot(qa[hh][rows], kb_ref[hh, :, pl.ds(col, ncols)])
                  for hh, rows, ncols in chains]
        ms, accs = ([], []), ([], [])
        for s, (hh, rows, ncols) in zip(logits, chains):
            m, acc = carry[hh]
            s = jnp.where(causal[rows, 0:ncols], s, -jnp.inf)
            m_new, acc_new = update(s, m[rows], acc[rows], hh, col, ncols)
            ms[hh].append(m_new)
            accs[hh].append(acc_new)
        return tuple((jnp.concatenate(ms[hh], axis=0), jnp.concatenate(accs[hh], axis=0))
                     for hh in range(2))

    init = tuple((jnp.full((tq, 1), -jnp.inf, F32), jnp.zeros((tq, LANES), F32)) for _ in range(2))
    carry = lax.fori_loop(0, qi, lambda j, c: full_tiles([j], c), init)
    (_, acc0), (_, acc1) = diagonal_tile(carry)
    out0 = acc0 / acc0[:, HEAD_DIM:HEAD_DIM + 1]
    out1 = acc1 / acc1[:, 0:1]
    o_ref[...] = jnp.where((lane // HEAD_DIM) == 0, out0, out1).astype(BF16)


def _prompt_attn(layer, q, kt, vt, c, batch, seq):
    t = ATTN_TILE
    nq = seq // t
    pairs = D_ATTN // LANES
    kv_spec = pl.BlockSpec((None, 1, LANES, seq), lambda b, p, i: (layer, b, p, 0))
    return pl.pallas_call(
        _attn_kernel,
        grid=(batch, pairs, nq),
        in_specs=[
            pl.BlockSpec((t, LANES), lambda b, p, i: (b * nq + i, p)),
            kv_spec,
            kv_spec,
            pl.BlockSpec((1, N_HEADS, seq), lambda b, p, i: (b, 0, 0)),
        ],
        out_specs=pl.BlockSpec((t, LANES), lambda b, p, i: (b * nq + i, p)),
        out_shape=jax.ShapeDtypeStruct((batch * seq, D_ATTN), BF16),
        scratch_shapes=[pltpu.VMEM((2, LANES, seq), BF16), pltpu.VMEM((2, LANES, seq), BF16)],
        compiler_params=pltpu.CompilerParams(
            dimension_semantics=("arbitrary", "arbitrary", "arbitrary"),
            vmem_limit_bytes=VMEM_LIMIT),
        name="prompt_attn",
    )(q, kt, vt, c)


def _mix_kernel(alpha, x_ref, a_ref, cb_ref, wg_ref, bg_ref, wpa_ref, wpc_ref, wo_ref,
                g_ref, b_ref, o_ref, m_ref):
    subs = _row_subtiles(x_ref.shape[0])
    for rows in subs:
        xb = x_ref[rows].astype(BF16)
        a = a_ref[rows]
        cb = cb_ref[rows]
        for c in range(0, D_MODEL, MIX_COLS):
            cols = slice(c, c + MIX_COLS)
            gcols = slice(D_MODEL + c, D_MODEL + c + MIX_COLS)
            ga = jax.nn.sigmoid(_dot(xb, wg_ref[:, cols]) + bg_ref[:, cols])
            gc = jax.nn.sigmoid(_dot(xb, wg_ref[:, gcols]) + bg_ref[:, gcols])
            m = ga * _dot(a, wpa_ref[:, cols]) + gc * _dot(cb, wpc_ref[:, cols])
            m_ref[rows, cols] = m.astype(BF16)
    for rows in subs:
        tm = _dot(m_ref[rows], wo_ref[...])
        o_ref[rows] = _layer_norm(alpha * x_ref[rows] + tm, g_ref[...], b_ref[...])


def _mix(layer, x, a, cb, wg, bg, wpa, wpc, wo, g, b, alpha, rows):
    n = x.shape[0]
    row_map = lambda i: (i, 0)
    return pl.pallas_call(
        functools.partial(_mix_kernel, alpha),
        grid=(n // rows,),
        in_specs=[
            pl.BlockSpec((rows, D_MODEL), row_map),
            pl.BlockSpec((rows, D_ATTN), row_map),
            pl.BlockSpec((rows, D_CONV), row_map),
            _resident(wg), _resident(bg),
            _resident(wpa, layer), _resident(wpc, layer), _resident(wo, layer),
            _resident(g), _resident(b),
        ],
        out_specs=pl.BlockSpec((rows, D_MODEL), row_map),
        out_shape=jax.ShapeDtypeStruct((n, D_MODEL), F32),
        scratch_shapes=[pltpu.VMEM((rows, D_MODEL), BF16)],
        compiler_params=pltpu.CompilerParams(
            dimension_semantics=("arbitrary",), vmem_limit_bytes=VMEM_LIMIT),
        name="mix",
    )(x, a, cb, wg, bg, wpa, wpc, wo, g, b)


def _ffn_body(alpha, x_ref, wgu_ref, wd_ref, g_ref, b_ref, o_ref, act_ref):
    subs = _row_subtiles(x_ref.shape[0])
    for rows in subs:
        xb = x_ref[rows].astype(BF16)
        for c in range(0, D_FF, FFN_COLS):
            gate = _dot(xb, wgu_ref[:, c:c + FFN_COLS])
            up = _dot(xb, wgu_ref[:, D_FF + c:D_FF + c + FFN_COLS])
            act_ref[rows, c:c + FFN_COLS] = (jax.nn.silu(gate) * up).astype(BF16)
    for rows in subs:
        f = _dot(act_ref[rows], wd_ref[...])
        o_ref[rows] = _layer_norm(alpha * x_ref[rows] + f, g_ref[...], b_ref[...])


def _resident(a, layer=None):
    if layer is None:
        return pl.BlockSpec(a.shape, lambda *_: (0,) * a.ndim, pipeline_mode=pl.Buffered(1))
    return pl.BlockSpec((None,) + a.shape[1:], lambda *_: (layer,) + (0,) * (a.ndim - 1),
                        pipeline_mode=pl.Buffered(1))


def _ffn(layer, x, wgu, wd, g, b, alpha, rows):
    n = x.shape[0]
    return pl.pallas_call(
        functools.partial(_ffn_body, alpha),
        grid=(n // rows,),
        in_specs=[
            pl.BlockSpec((rows, D_MODEL), lambda i: (i, 0)),
            _resident(wgu, layer), _resident(wd, layer), _resident(g), _resident(b),
        ],
        out_specs=pl.BlockSpec((rows, D_MODEL), lambda i: (i, 0)),
        out_shape=jax.ShapeDtypeStruct((n, D_MODEL), F32),
        scratch_shapes=[pltpu.VMEM((rows, D_FF), BF16)],
        compiler_params=pltpu.CompilerParams(
            dimension_semantics=("arbitrary",), vmem_limit_bytes=VMEM_LIMIT),
        name="ffn",
    )(x, wgu, wd, g, b)


def _sample_proj_kernel(x_ref, wn_ref, wkv_ref, wf_ref, bf_ref, cw_ref, h0_ref, h1_ref,
                        q_ref, k_ref, v_ref, lf_ref, cb_ref, u_ref):
    xb = x_ref[...].astype(BF16)
    z = _dot(xb, wn_ref[...])
    q_ref[...] = z[:, 0:D_ATTN] * SCALE
    kv = _dot_nt(xb, wkv_ref[...])
    k_ref[...] = kv[:, 0:D_ATTN]
    v_ref[...] = kv[:, D_ATTN:2 * D_ATTN]
    lf_ref[...] = jax.nn.log_sigmoid(_dot_nt(wf_ref[...], xb) + bf_ref[...])
    h = z[:, D_ATTN:D_ATTN + D_CONV]
    gb = z[:, D_ATTN + D_CONV:D_ATTN + 2 * D_CONV]
    gc = z[:, D_ATTN + 2 * D_CONV:D_ATTN + 3 * D_CONV]
    u = gc * h
    cw = cw_ref[...]
    cv = cw[0:1] * h0_ref[...] + cw[1:2] * h1_ref[...] + cw[2:3] * u
    cb_ref[...] = (gb * cv).astype(BF16)
    u_ref[...] = u


def _sample_proj(x, wn, wkvt, wft, bf, cw, h0, h1):
    n = x.shape[0]
    args = (x, wn, wkvt, wft, bf, cw, h0, h1)
    full = lambda a: pl.BlockSpec(a.shape, lambda i: (0,) * a.ndim)
    out_shape = [
        jax.ShapeDtypeStruct((n, D_ATTN), F32),
        jax.ShapeDtypeStruct((n, D_ATTN), F32),
        jax.ShapeDtypeStruct((n, D_ATTN), F32),
        jax.ShapeDtypeStruct((N_HEADS, n), F32),
        jax.ShapeDtypeStruct((n, D_CONV), BF16),
        jax.ShapeDtypeStruct((n, D_CONV), F32),
    ]
    return pl.pallas_call(
        _sample_proj_kernel,
        grid=(1,),
        in_specs=[full(a) for a in args],
        out_specs=[full(s) for s in out_shape],
        out_shape=out_shape,
        compiler_params=pltpu.CompilerParams(
            dimension_semantics=("arbitrary",), vmem_limit_bytes=VMEM_LIMIT),
        name="sample_proj",
    )(*args)


def _head_rows(h):
    return slice(h * HEAD_DIM, (h + 1) * HEAD_DIM)


def _paged_logits(h, qb_ref, k_refs, s_ref):
    qh = qb_ref[_head_rows(h), :]
    for j, k in enumerate(k_refs):
        prod = k[_head_rows(h), :] * qh
        s_ref[h:h + 1, j * LANES:(j + 1) * LANES] = jnp.sum(prod, axis=0, keepdims=True)


def _paged_softmax(s_ref, p_ref, a_ref, lf_refs, tri_ref, m_ref, l_ref, carry_ref):
    pages = len(lf_refs)
    m = m_ref[...]
    carry = carry_ref[...]
    lf = jnp.concatenate([r[...] for r in lf_refs], axis=0)
    n = pages * N_HEADS
    r = _dot(jnp.concatenate(_split3(lf), axis=0), tri_ref[...])
    r = r[0:n] + r[n:2 * n] + r[2 * n:3 * n]
    cs = r[:, 0:LANES]
    tot = r[:, LANES:2 * LANES]
    c_pages = []
    for j in range(pages):
        c_pages.append(cs[j * N_HEADS:(j + 1) * N_HEADS] + carry)
        carry = carry + tot[j * N_HEADS:(j + 1) * N_HEADS]
    s = s_ref[...] - jnp.concatenate(c_pages, axis=1)
    m_new = jnp.maximum(m, jnp.max(s, axis=-1, keepdims=True))
    alpha = jnp.exp(m - m_new)
    p = jnp.exp(s - m_new)
    p_ref[...] = p
    a_ref[...] = alpha
    m_ref[...] = m_new
    l_ref[...] = alpha * l_ref[...] + jnp.sum(p, axis=-1, keepdims=True)
    carry_ref[...] = carry


def _paged_values(h, p_ref, a_ref, v_refs, acc_ref):
    acc = acc_ref[_head_rows(h), :] * a_ref[h:h + 1, :]
    for j, v in enumerate(v_refs):
        acc = acc + v[_head_rows(h), :] * p_ref[h:h + 1, j * LANES:(j + 1) * LANES]
    acc_ref[_head_rows(h), :] = acc


def _paged_finish(qc_ref, knc_ref, vnc_ref, lfn_ref, m_ref, l_ref, acc_ref, carry_ref, o_ref):
    m = m_ref[...]
    qk = qc_ref[0] * knc_ref[0]
    s_new = jnp.concatenate([jnp.sum(qk[_head_rows(h)], axis=0, keepdims=True)
                             for h in range(N_HEADS)], axis=0)
    s_new = s_new - (carry_ref[:, 0:1] + lfn_ref[0])
    m_fin = jnp.maximum(m, s_new)
    alpha = jnp.exp(m - m_fin)
    p_new = jnp.exp(s_new - m_fin)
    l_fin = alpha * l_ref[...] + p_new
    for h in range(N_HEADS):
        rows = _head_rows(h)
        num = (alpha[h:h + 1, :] * jnp.sum(acc_ref[rows, :], axis=-1, keepdims=True)
               + p_new[h:h + 1, :] * vnc_ref[0, rows, :])
        o_ref[0, rows, :] = num / l_fin[h:h + 1, :]


def _page_copies(layer, pt_ref, seq, quarter, slot, caches, bufs, sems):
    pages = bufs[0].shape[1]
    copies = []
    for j in range(pages):
        page = pt_ref[seq, quarter * pages + j]
        for kind, (cache, buf) in enumerate(zip(caches, bufs)):
            copies.append(pltpu.make_async_copy(cache.at[layer, page], _page_view(buf, slot, j),
                                                sems.at[slot, kind]))
    return copies


def _page_view(buf, slot, j):
    return buf.at[slot, j, pl.ds(0, buf.shape[2] - PAGE_SKEW_ROWS)]


def _ffn_paged_kernel(alpha, layer, pt_ref, x_ref, wgu_ref, wd_ref, g_ref, b_ref,
                      q_ref, kn_ref, vn_ref, lfn_ref, tri_ref, ck_ref, cv_ref, clf_ref,
                      y_ref, o_ref, act_ref, part_ref, qb_ref, s_ref, p_ref, a_ref,
                      m_ref, l_ref, acc_ref, carry_ref, kbuf, vbuf, lfbuf, sems):
    seq = pl.program_id(0)
    quarter = pl.program_id(1)
    n_seq = pl.num_programs(0)
    pages = kbuf.shape[1]
    caches, bufs = (ck_ref, cv_ref, clf_ref), (kbuf, vbuf, lfbuf)
    copies = functools.partial(_page_copies, layer, pt_ref, caches=caches, bufs=bufs, sems=sems)
    subs = _row_subtiles(x_ref.shape[0])

    def gate_up(c0, c1):
        def piece(rows, c):
            xb = x_ref[rows].astype(BF16)
            gate = _dot(xb, wgu_ref[:, c:c + FFN_COLS])
            up = _dot(xb, wgu_ref[:, D_FF + c:D_FF + c + FFN_COLS])
            act_ref[rows, c:c + FFN_COLS] = (jax.nn.silu(gate) * up).astype(BF16)
        return [functools.partial(piece, rows, c)
                for rows in subs for c in range(c0, c1, FFN_COLS)]

    def down(k0, k1, first=False, last=False):
        def piece(rows):
            part = _dot(act_ref[rows, k0:k1], wd_ref[k0:k1, :])
            if first:
                part_ref[rows] = part
            elif not last:
                part_ref[rows] += part
            else:
                y_ref[rows] = _layer_norm(alpha * x_ref[rows] + (part_ref[rows] + part),
                                          g_ref[...], b_ref[...])
        return [functools.partial(piece, rows) for rows in subs]

    c2, c4 = 4 * FFN_COLS, 8 * FFN_COLS
    ffn_work = [
        gate_up(0, c2),
        gate_up(c2, c4),
        gate_up(c4, D_FF) + down(0, c2, first=True),
        down(c2, c4) + down(c4, D_FF, last=True),
    ]
    last = len(ffn_work) - 1
    for k, pieces in enumerate(ffn_work):
        @pl.when(quarter == k)
        def _(k=k, pieces=pieces):
            slot = k % 2
            if k == 0:
                @pl.when(seq == 0)
                def _():
                    for c in copies(seq, k, slot):
                        c.start()
            if k < last:
                for c in copies(seq, k + 1, 1 - slot):
                    c.start()
            else:
                @pl.when(seq + 1 < n_seq)
                def _():
                    for c in copies(seq + 1, 0, 1 - slot):
                        c.start()
            for c in copies(seq, k, slot):
                c.wait()
            k_refs = [_page_view(kbuf, slot, j) for j in range(pages)]
            v_refs = [_page_view(vbuf, slot, j) for j in range(pages)]
            lf_refs = [_page_view(lfbuf, slot, j) for j in range(pages)]
            attn_work = (
                [functools.partial(_paged_logits, h, qb_ref, k_refs, s_ref)
                 for h in range(N_HEADS)]
                + [functools.partial(_paged_softmax, s_ref, p_ref, a_ref, lf_refs, tri_ref,
                                     m_ref, l_ref, carry_ref)]
                + [functools.partial(_paged_values, h, p_ref, a_ref, v_refs, acc_ref)
                   for h in range(N_HEADS)])
            if k == 0:
                m_ref[...] = jnp.full_like(m_ref, -jnp.inf)
                l_ref[...] = jnp.zeros_like(l_ref)
                acc_ref[...] = jnp.zeros_like(acc_ref)
                carry_ref[...] = jnp.zeros_like(carry_ref)
                qb_ref[...] = jnp.broadcast_to(q_ref[0], qb_ref.shape)
            for piece in attn_work + pieces:
                piece()
            if k == last:
                _paged_finish(q_ref, kn_ref, vn_ref, lfn_ref, m_ref, l_ref, acc_ref, carry_ref,
                              o_ref)


def _ffn_paged(layer, page_table, x, wgu, wd, g, b, alpha, q, kn, vn, lfn, tri,
               cache_kt, cache_vt, cache_lft):
    n, n_pages = page_table.shape
    quarters = 4
    assert n_pages % quarters == 0 and x.shape[0] % n == 0
    pages = n_pages // quarters
    rows = x.shape[0] // n
    assert rows % SUBLANES == 0
    x_spec = pl.BlockSpec((rows, D_MODEL), lambda s, k, pt: (s, 0))
    row = pl.BlockSpec((1, D_ATTN, 1), lambda s, k, pt: (s, 0, 0))

    hbm = pl.BlockSpec(memory_space=pl.ANY)
    in_specs = [x_spec, _resident(wgu, layer), _resident(wd, layer), _resident(g), _resident(b),
                row, row, row,
                pl.BlockSpec((1, N_HEADS, 1), lambda s, k, pt: (s, 0, 0)),
                _resident(tri), hbm, hbm, hbm]
    grid_spec = pltpu.PrefetchScalarGridSpec(
        num_scalar_prefetch=1,
        grid=(n, quarters),
        in_specs=in_specs,
        out_specs=[x_spec, row],
        scratch_shapes=[
            pltpu.VMEM((rows, D_FF), BF16),
            pltpu.VMEM((rows, D_MODEL), F32),
            pltpu.VMEM((D_ATTN, LANES), F32),
            pltpu.VMEM((N_HEADS, pages * PAGE_SIZE), F32),
            pltpu.VMEM((N_HEADS, pages * PAGE_SIZE), F32),
            pltpu.VMEM((N_HEADS, 1), F32),
            pltpu.VMEM((N_HEADS, 1), F32),
            pltpu.VMEM((N_HEADS, 1), F32),
            pltpu.VMEM((D_ATTN, LANES), F32),
            pltpu.VMEM((N_HEADS, LANES), F32),
            pltpu.VMEM((2, pages, D_ATTN + PAGE_SKEW_ROWS, PAGE_SIZE), F32),
            pltpu.VMEM((2, pages, D_ATTN + PAGE_SKEW_ROWS, PAGE_SIZE), F32),
            pltpu.VMEM((2, pages, N_HEADS + PAGE_SKEW_ROWS, PAGE_SIZE), F32),
            pltpu.SemaphoreType.DMA((2, 3)),
        ],
    )
    return pl.pallas_call(
        functools.partial(_ffn_paged_kernel, alpha, layer),
        grid_spec=grid_spec,
        out_shape=[jax.ShapeDtypeStruct(x.shape, F32),
                   jax.ShapeDtypeStruct((n, D_ATTN, 1), F32)],
        compiler_params=pltpu.CompilerParams(
            dimension_semantics=("arbitrary", "arbitrary"), vmem_limit_bytes=VMEM_LIMIT),
        name="ffn_paged",
    )(page_table, x, wgu, wd, g, b, q, kn, vn, lfn, tri, cache_kt, cache_vt, cache_lft)


def kernel(x_prompt, x_sample, cache_k, cache_v, cache_logf, state_conv, page_table, w_in, b_f, b_gate, conv_w, w_attn_proj, w_conv_proj, w_out, ln1_g, ln1_b, w_gate_up, w_down, ln2_g, ln2_b):
    batch, seq, _ = x_prompt.shape
    n_dec = x_sample.shape[0]
    depth = w_in.shape[0]
    n_pool = cache_k.shape[1]
    alpha = (2 * depth) ** 0.25

    cache_kt = jnp.transpose(cache_k, (0, 1, 3, 4, 2)).reshape(depth, n_pool, D_ATTN, PAGE_SIZE)
    cache_vt = jnp.transpose(cache_v, (0, 1, 3, 4, 2)).reshape(depth, n_pool, D_ATTN, PAGE_SIZE)
    cache_lft = jnp.transpose(cache_logf, (0, 1, 3, 2))
    tri = jnp.triu(jnp.ones((LANES, LANES), F32)).astype(BF16)
    tri_ones = jnp.concatenate([tri, jnp.ones((LANES, LANES), BF16)], axis=1)

    xp = x_prompt.reshape(batch * seq, D_MODEL)
    xs = x_sample.reshape(n_dec, D_MODEL)
    kv_all, lp, cp, ksm, vsm, lsm, csm = (), [], [], [], [], [], []
    wpa, wpc, wo = w_attn_proj.astype(BF16), w_conv_proj.astype(BF16), w_out.astype(BF16)
    wgu, wd = w_gate_up.astype(BF16), w_down.astype(BF16)
    for l in range(depth):
        wt = jnp.transpose(w_in[l])
        wn = jnp.concatenate([w_in[l][:, OFF_Q:OFF_K], w_in[l][:, OFF_H:OFF_G]], axis=1).astype(BF16)
        wkvt = wt[OFF_K:OFF_F].astype(BF16)
        wft = wt[OFF_F:OFF_H].astype(BF16)
        wg = w_in[l][:, OFF_G:].astype(BF16)
        bf = b_f[l].reshape(N_HEADS, 1)
        bg = b_gate[l].reshape(1, 2 * D_MODEL)
        cw = conv_w[l]
        g1, b1 = ln1_g[l].reshape(1, D_MODEL), ln1_b[l].reshape(1, D_MODEL)
        g2, b2 = ln2_g[l].reshape(1, D_MODEL), ln2_b[l].reshape(1, D_MODEL)

        q, kt, vt, lf, c, cb, st = _prompt_proj(l, depth, xp, wn, wkvt, wft, bf, cw, tri,
                                                batch, seq, kv_all)
        kv_all = (kt, vt)
        a = _prompt_attn(l, q, kt, vt, c, batch, seq)
        x1 = _mix(l, xp, a, cb, wg, bg, wpa, wpc, wo, g1, b1, alpha, MIX_ROWS)
        lp.append(lf); cp.append(st)
        h0, h1 = state_conv[l, :, 0, :], state_conv[l, :, 1, :]
        qs, ks, vs, lfs, cbs, us = _sample_proj(xs, wn, wkvt, wft, bf, cw, h0, h1)

        xp, a_s = _ffn_paged(l, page_table, x1, wgu, wd, g2, b2, alpha,
                             qs.reshape(n_dec, D_ATTN, 1), ks.reshape(n_dec, D_ATTN, 1),
                             vs.reshape(n_dec, D_ATTN, 1),
                             jnp.transpose(lfs).reshape(n_dec, N_HEADS, 1),
                             tri_ones, cache_kt, cache_vt, cache_lft)

        x1s = _mix(l, xs, a_s.reshape(n_dec, D_ATTN).astype(BF16), cbs, wg, bg, wpa, wpc, wo,
                   g1, b1, alpha, n_dec)
        xs = _ffn(l, x1s, wgu, wd, g2, b2, alpha, n_dec)
        ksm.append(ks); vsm.append(vs); lsm.append(lfs); csm.append(jnp.stack([h1, us], axis=1))

    def heads_last(t):
        return jnp.transpose(t.reshape(depth, batch, N_HEADS, HEAD_DIM, seq), (0, 1, 4, 2, 3))

    y_prompt = xp.reshape(batch, seq, D_MODEL)
    y_sample = xs.reshape(n_dec, 1, D_MODEL)
    k_prompt = heads_last(kv_all[0])
    v_prompt = heads_last(kv_all[1])
    logf_prompt = jnp.transpose(jnp.stack(lp), (0, 1, 3, 2))
    conv_prompt = jnp.stack(cp)
    k_sample = jnp.stack(ksm).reshape(depth, n_dec, 1, N_HEADS, HEAD_DIM)
    v_sample = jnp.stack(vsm).reshape(depth, n_dec, 1, N_HEADS, HEAD_DIM)
    logf_sample = jnp.transpose(jnp.stack(lsm), (0, 2, 1)).reshape(depth, n_dec, 1, N_HEADS)
    conv_sample = jnp.stack(csm)
    return (y_prompt, y_sample, k_prompt, v_prompt, logf_prompt, conv_prompt,
            k_sample, v_sample, logf_sample, conv_sample)
```

```python
import functools

import jax
import jax.numpy as jnp
from jax import lax
from jax.experimental import pallas as pl
from jax.experimental.pallas import tpu as pltpu

D_MODEL = 1024
N_HEADS = 8
HEAD_DIM = 64
D_ATTN = N_HEADS * HEAD_DIM
D_CONV = 512
CONV_W = 3
D_FF = 2816
PAGE_SIZE = 128
LN_EPS = 1e-5
SCALE = HEAD_DIM ** -0.5

OFF_Q = 0
OFF_K = OFF_Q + D_ATTN
OFF_V = OFF_K + D_ATTN
OFF_F = OFF_V + D_ATTN
OFF_H = OFF_F + N_HEADS
OFF_B = OFF_H + D_CONV
OFF_C = OFF_B + D_CONV
OFF_G = OFF_C + D_CONV

LANES = 128
SUBLANES = 8
VMEM_LIMIT = 56 * 1024 * 1024

PROJ_ROWS = 1024
PROJ_COLS = 256
ATTN_TILE = 1024
MIX_ROWS = 1024
MIX_COLS = 256
SUB_ROWS = 256
FFN_COLS = 256
PAGE_SKEW_ROWS = SUBLANES

BF16 = jnp.bfloat16
F32 = jnp.float32

_NT = (((1,), (1,)), ((), ()))


def _dot(a, b):
    return jnp.dot(a, b, preferred_element_type=F32)


def _dot_nt(a, b):
    return lax.dot_general(a, b, _NT, preferred_element_type=F32)


def _split3(x):
    hi = x.astype(BF16)
    r1 = x - hi.astype(F32)
    mid = r1.astype(BF16)
    lo = (r1 - mid.astype(F32)).astype(BF16)
    return hi, mid, lo


def _lane_cumsum(x, tri):
    rows = x.shape[0]
    parts = jnp.concatenate(_split3(x), axis=0)
    cs = _dot(parts, tri)
    return cs[:rows] + cs[rows:2 * rows] + cs[2 * rows:]


def _row_subtiles(rows):
    sub = min(rows, SUB_ROWS)
    return [slice(r, r + sub) for r in range(0, rows, sub)]


def _layer_norm(y, g, b):
    mu = jnp.mean(y, axis=-1, keepdims=True)
    d = y - mu
    var = jnp.mean(d * d, axis=-1, keepdims=True)
    return d * lax.rsqrt(var + LN_EPS) * g + b


def _proj_kernel(n_aliased, x_ref, wn_ref, wkv_ref, wf_ref, bf_ref, cw_ref, tri_ref, *refs):
    q_ref, kt_ref, vt_ref, lf_ref, c_ref, cb_ref, st_ref, hist_ref, carry_ref = refs[n_aliased:]
    si = pl.program_id(1)
    rows = x_ref.shape[0]

    @pl.when(si == 0)
    def _():
        hist_ref[...] = jnp.zeros_like(hist_ref)
        carry_ref[...] = jnp.zeros_like(carry_ref)

    xb = x_ref[...].astype(BF16)
    pc = PROJ_COLS
    for c in range(0, D_ATTN, pc):
        q_ref[:, c:c + pc] = (_dot(xb, wn_ref[:, c:c + pc]) * SCALE).astype(BF16)
    for r in range(0, D_ATTN, pc):
        kt = _dot_nt(wkv_ref[r:r + pc], xb)
        vt = _dot_nt(wkv_ref[D_ATTN + r:D_ATTN + r + pc], xb)
        for slot in range(kt_ref.shape[0]):
            kt_ref[slot, 0, r:r + pc] = kt
            vt_ref[slot, 0, r:r + pc] = vt

    lf = jax.nn.log_sigmoid(_dot_nt(wf_ref[...], xb) + bf_ref[...])
    lf_ref[0] = lf
    carry = carry_ref[...]
    for j in range(rows // LANES):
        cs = _lane_cumsum(lf[:, j * LANES:(j + 1) * LANES], tri_ref[...]) + carry
        c_ref[0, :, j * LANES:(j + 1) * LANES] = cs
        carry = cs[:, LANES - 1:LANES]
    carry_ref[...] = carry

    row = lax.broadcasted_iota(jnp.int32, (SUBLANES, pc), 0)
    for c in range(0, D_CONV, pc):
        h = _dot(xb, wn_ref[:, D_ATTN + c:D_ATTN + c + pc])
        gb = _dot(xb, wn_ref[:, D_ATTN + D_CONV + c:D_ATTN + D_CONV + c + pc])
        gc = _dot(xb, wn_ref[:, D_ATTN + 2 * D_CONV + c:D_ATTN + 2 * D_CONV + c + pc])
        u = gc * h
        hist = hist_ref[:, c:c + pc]
        r1 = pltpu.roll(u, 1, 0)
        r2 = pltpu.roll(u, 2, 0)
        h1 = pltpu.roll(hist, 1, 0)
        h2 = pltpu.roll(hist, 2, 0)
        u1 = jnp.concatenate([jnp.where(row < 1, h1, r1[0:SUBLANES]), r1[SUBLANES:]], axis=0)
        u2 = jnp.concatenate([jnp.where(row < 2, h2, r2[0:SUBLANES]), r2[SUBLANES:]], axis=0)
        cw = cw_ref[:, c:c + pc]
        cv = cw[0:1] * u2 + cw[1:2] * u1 + cw[2:3] * u
        cb_ref[:, c:c + pc] = (gb * cv).astype(BF16)
        hist_ref[:, c:c + pc] = u[rows - SUBLANES:rows]

    @pl.when(si == pl.num_programs(1) - 1)
    def _():
        st_ref[0] = hist_ref[SUBLANES - (CONV_W - 1):SUBLANES, :]


def _prompt_proj(layer, depth, x, wn, wkvt, wft, bf, cw, tri, batch, seq, kv_all=()):
    rows = PROJ_ROWS
    ns = seq // rows
    row_map = lambda b, s: (b * ns + s, 0)
    seq_map = lambda b, s: (b, 0, s)
    if kv_all:
        kv_spec = pl.BlockSpec((1, 1, D_ATTN, rows), lambda b, s: (layer, b, 0, s))
    else:
        kv_spec = pl.BlockSpec((depth, 1, D_ATTN, rows), lambda b, s: (0, b, 0, s))
    consts = (wn, wkvt, wft, bf, cw, tri)
    return pl.pallas_call(
        functools.partial(_proj_kernel, len(kv_all)),
        grid=(batch, ns),
        in_specs=([pl.BlockSpec((rows, D_MODEL), row_map)] + [_resident(a) for a in consts]
                  + [pl.BlockSpec(memory_space=pl.ANY) for _ in kv_all]),
        out_specs=[
            pl.BlockSpec((rows, D_ATTN), row_map),
            kv_spec,
            kv_spec,
            pl.BlockSpec((1, N_HEADS, rows), seq_map),
            pl.BlockSpec((1, N_HEADS, rows), seq_map),
            pl.BlockSpec((rows, D_CONV), row_map),
            pl.BlockSpec((1, CONV_W - 1, D_CONV), lambda b, s: (b, 0, 0)),
        ],
        out_shape=[
            jax.ShapeDtypeStruct((batch * seq, D_ATTN), BF16),
            jax.ShapeDtypeStruct((depth, batch, D_ATTN, seq), F32),
            jax.ShapeDtypeStruct((depth, batch, D_ATTN, seq), F32),
            jax.ShapeDtypeStruct((batch, N_HEADS, seq), F32),
            jax.ShapeDtypeStruct((batch, N_HEADS, seq), F32),
            jax.ShapeDtypeStruct((batch * seq, D_CONV), BF16),
            jax.ShapeDtypeStruct((batch, CONV_W - 1, D_CONV), F32),
        ],
        input_output_aliases={1 + len(consts) + i: 1 + i for i in range(len(kv_all))},
        scratch_shapes=[
            pltpu.VMEM((SUBLANES, D_CONV), F32),
            pltpu.VMEM((N_HEADS, 1), F32),
        ],
        compiler_params=pltpu.CompilerParams(
            dimension_semantics=("arbitrary", "arbitrary"), vmem_limit_bytes=VMEM_LIMIT),
        name="prompt_proj",
    )(x, *consts, *kv_all)


def _attn_kernel(q_ref, kt_ref, vt_ref, c_ref, o_ref, kb_ref, vb_ref):
    pair = pl.program_id(1)
    qi = pl.program_id(2)
    tq = q_ref.shape[0]
    tk = tq
    seq = kt_ref.shape[2]

    @pl.when(qi == 0)
    def _():
        kt = kt_ref[0]
        vt = vt_ref[0]
        rowi = lax.broadcasted_iota(jnp.int32, (HEAD_DIM, seq), 0)
        ones_row = jnp.where(rowi == 0, 1.0, 0.0).astype(BF16)
        extra = []
        for hh in range(2):
            hi, mid, lo = _split3(-c_ref[0, pl.ds(2 * pair + hh, 1), :])
            e = jnp.where(rowi == 0, hi.astype(F32),
                          jnp.where(rowi == 1, mid.astype(F32),
                                    jnp.where(rowi == 2, lo.astype(F32), 0.0)))
            extra.append(e.astype(BF16))
        kb_ref[0, 0:HEAD_DIM] = kt[0:HEAD_DIM].astype(BF16)
        kb_ref[0, HEAD_DIM:] = extra[0]
        kb_ref[1, 0:HEAD_DIM] = extra[1]
        kb_ref[1, HEAD_DIM:] = kt[HEAD_DIM:].astype(BF16)
        vb_ref[0, 0:HEAD_DIM] = vt[0:HEAD_DIM].astype(BF16)
        vb_ref[0, HEAD_DIM:] = ones_row
        vb_ref[1, 0:HEAD_DIM] = ones_row
        vb_ref[1, HEAD_DIM:] = vt[HEAD_DIM:].astype(BF16)

    q2 = q_ref[...]
    lane = lax.broadcasted_iota(jnp.int32, (1, LANES), 1)
    causal = (lax.broadcasted_iota(jnp.int32, (tq, tk), 1)
              <= lax.broadcasted_iota(jnp.int32, (tq, tk), 0))

    qa = []
    for hh in range(2):
        base = (1 - hh) * HEAD_DIM
        ones3 = jnp.where((lane >= base) & (lane < base + 3), 1.0, 0.0).astype(BF16)
        qa.append(jnp.where((lane // HEAD_DIM) == hh, q2, ones3))

    def update(s, m, acc, hh, col, ncols):
        m_new = jnp.maximum(m, jnp.max(s, axis=-1, keepdims=True))
        alpha = jnp.exp(m - m_new)
        p = jnp.exp(s - m_new).astype(BF16)
        pv = _dot_nt(p, vb_ref[hh, :, pl.ds(col, ncols)])
        return m_new, alpha * acc + pv

    def full_tiles(js, carry):
        cols = [pl.multiple_of(j * tk, tk) for j in js]
        logits = [[_dot(qa[hh], kb_ref[hh, :, pl.ds(col, tk)]) for hh in range(2)]
                  for col in cols]
        carry = list(carry)
        for col, s_pair in zip(cols, logits):
            for hh in range(2):
                carry[hh] = update(s_pair[hh], *carry[hh], hh, col, tk)
        return tuple(carry)

    def diagonal_tile(carry):
        col = pl.multiple_of(qi * tk, tk)
        half = tq // 2
        chains = [(hh, slice(r, r + half), r + half) for hh in range(2) for r in (0, half)]
        logits = [_dot(qa[hh][rows], kb_ref[hh, :, pl.ds(col, ncols)])
                  for hh, rows, ncols in chains]
        ms, accs = ([], []), ([], [])
        for s, (hh, rows, ncols) in zip(logits, chains):
            m, acc = carry[hh]
            s = jnp.where(causal[rows, 0:ncols], s, -jnp.inf)
            m_new, acc_new = update(s, m[rows], acc[rows], hh, col, ncols)
            ms[hh].append(m_new)
            accs[hh].append(acc_new)
        return tuple((jnp.concatenate(ms[hh], axis=0), jnp.concatenate(accs[hh], axis=0))
                     for hh in range(2))

    init = tuple((jnp.full((tq, 1), -jnp.inf, F32), jnp.zeros((tq, LANES), F32)) for _ in range(2))
    carry = lax.fori_loop(0, qi, lambda j, c: full_tiles([j], c), init)
    (_, acc0), (_, acc1) = diagonal_tile(carry)
    out0 = acc0 / acc0[:, HEAD_DIM:HEAD_DIM + 1]
    out1 = acc1 / acc1[:, 0:1]
    o_ref[...] = jnp.where((lane // HEAD_DIM) == 0, out0, out1).astype(BF16)


def _prompt_attn(layer, q, kt, vt, c, batch, seq):
    t = ATTN_TILE
    nq = seq // t
    pairs = D_ATTN // LANES
    kv_spec = pl.BlockSpec((None, 1, LANES, seq), lambda b, p, i: (layer, b, p, 0))
    return pl.pallas_call(
        _attn_kernel,
        grid=(batch, pairs, nq),
        in_specs=[
            pl.BlockSpec((t, LANES), lambda b, p, i: (b * nq + i, p)),
            kv_spec,
            kv_spec,
            pl.BlockSpec((1, N_HEADS, seq), lambda b, p, i: (b, 0, 0)),
        ],
        out_specs=pl.BlockSpec((t, LANES), lambda b, p, i: (b * nq + i, p)),
        out_shape=jax.ShapeDtypeStruct((batch * seq, D_ATTN), BF16),
        scratch_shapes=[pltpu.VMEM((2, LANES, seq), BF16), pltpu.VMEM((2, LANES, seq), BF16)],
        compiler_params=pltpu.CompilerParams(
            dimension_semantics=("arbitrary", "arbitrary", "arbitrary"),
            vmem_limit_bytes=VMEM_LIMIT),
        name="prompt_attn",
    )(q, kt, vt, c)


def _mix_kernel(alpha, x_ref, a_ref, cb_ref, wg_ref, bg_ref, wpa_ref, wpc_ref, wo_ref,
                g_ref, b_ref, o_ref, m_ref):
    subs = _row_subtiles(x_ref.shape[0])
    for rows in subs:
        xb = x_ref[rows].astype(BF16)
        a = a_ref[rows]
        cb = cb_ref[rows]
        for c in range(0, D_MODEL, MIX_COLS):
            cols = slice(c, c + MIX_COLS)
            gcols = slice(D_MODEL + c, D_MODEL + c + MIX_COLS)
            ga = jax.nn.sigmoid(_dot(xb, wg_ref[:, cols]) + bg_ref[:, cols])
            gc = jax.nn.sigmoid(_dot(xb, wg_ref[:, gcols]) + bg_ref[:, gcols])
            m = ga * _dot(a, wpa_ref[:, cols]) + gc * _dot(cb, wpc_ref[:, cols])
            m_ref[rows, cols] = m.astype(BF16)
    for rows in subs:
        tm = _dot(m_ref[rows], wo_ref[...])
        o_ref[rows] = _layer_norm(alpha * x_ref[rows] + tm, g_ref[...], b_ref[...])


def _mix(layer, x, a, cb, wg, bg, wpa, wpc, wo, g, b, alpha, rows):
    n = x.shape[0]
    row_map = lambda i: (i, 0)
    return pl.pallas_call(
        functools.partial(_mix_kernel, alpha),
        grid=(n // rows,),
        in_specs=[
            pl.BlockSpec((rows, D_MODEL), row_map),
            pl.BlockSpec((rows, D_ATTN), row_map),
            pl.BlockSpec((rows, D_CONV), row_map),
            _resident(wg), _resident(bg),
            _resident(wpa, layer), _resident(wpc, layer), _resident(wo, layer),
            _resident(g), _resident(b),
        ],
        out_specs=pl.BlockSpec((rows, D_MODEL), row_map),
        out_shape=jax.ShapeDtypeStruct((n, D_MODEL), F32),
        scratch_shapes=[pltpu.VMEM((rows, D_MODEL), BF16)],
        compiler_params=pltpu.CompilerParams(
            dimension_semantics=("arbitrary",), vmem_limit_bytes=VMEM_LIMIT),
        name="mix",
    )(x, a, cb, wg, bg, wpa, wpc, wo, g, b)


def _ffn_body(alpha, x_ref, wgu_ref, wd_ref, g_ref, b_ref, o_ref, act_ref):
    subs = _row_subtiles(x_ref.shape[0])
    for rows in subs:
        xb = x_ref[rows].astype(BF16)
        for c in range(0, D_FF, FFN_COLS):
            gate = _dot(xb, wgu_ref[:, c:c + FFN_COLS])
            up = _dot(xb, wgu_ref[:, D_FF + c:D_FF + c + FFN_COLS])
            act_ref[rows, c:c + FFN_COLS] = (jax.nn.silu(gate) * up).astype(BF16)
    for rows in subs:
        f = _dot(act_ref[rows], wd_ref[...])
        o_ref[rows] = _layer_norm(alpha * x_ref[rows] + f, g_ref[...], b_ref[...])


def _resident(a, layer=None):
    if layer is None:
        return pl.BlockSpec(a.shape, lambda *_: (0,) * a.ndim, pipeline_mode=pl.Buffered(1))
    return pl.BlockSpec((None,) + a.shape[1:], lambda *_: (layer,) + (0,) * (a.ndim - 1),
                        pipeline_mode=pl.Buffered(1))


def _ffn(layer, x, wgu, wd, g, b, alpha, rows):
    n = x.shape[0]
    return pl.pallas_call(
        functools.partial(_ffn_body, alpha),
        grid=(n // rows,),
        in_specs=[
            pl.BlockSpec((rows, D_MODEL), lambda i: (i, 0)),
            _resident(wgu, layer), _resident(wd, layer), _resident(g), _resident(b),
        ],
        out_specs=pl.BlockSpec((rows, D_MODEL), lambda i: (i, 0)),
        out_shape=jax.ShapeDtypeStruct((n, D_MODEL), F32),
        scratch_shapes=[pltpu.VMEM((rows, D_FF), BF16)],
        compiler_params=pltpu.CompilerParams(
            dimension_semantics=("arbitrary",), vmem_limit_bytes=VMEM_LIMIT),
        name="ffn",
    )(x, wgu, wd, g, b)


def _sample_proj_kernel(x_ref, wn_ref, wkv_ref, wf_ref, bf_ref, cw_ref, h0_ref, h1_ref,
                        q_ref, k_ref, v_ref, lf_ref, cb_ref, u_ref):
    xb = x_ref[...].astype(BF16)
    z = _dot(xb, wn_ref[...])
    q_ref[...] = z[:, 0:D_ATTN] * SCALE
    kv = _dot_nt(xb, wkv_ref[...])
    k_ref[...] = kv[:, 0:D_ATTN]
    v_ref[...] = kv[:, D_ATTN:2 * D_ATTN]
    lf_ref[...] = jax.nn.log_sigmoid(_dot_nt(wf_ref[...], xb) + bf_ref[...])
    h = z[:, D_ATTN:D_ATTN + D_CONV]
    gb = z[:, D_ATTN + D_CONV:D_ATTN + 2 * D_CONV]
    gc = z[:, D_ATTN + 2 * D_CONV:D_ATTN + 3 * D_CONV]
    u = gc * h
    cw = cw_ref[...]
    cv = cw[0:1] * h0_ref[...] + cw[1:2] * h1_ref[...] + cw[2:3] * u
    cb_ref[...] = (gb * cv).astype(BF16)
    u_ref[...] = u


def _sample_proj(x, wn, wkvt, wft, bf, cw, h0, h1):
    n = x.shape[0]
    args = (x, wn, wkvt, wft, bf, cw, h0, h1)
    full = lambda a: pl.BlockSpec(a.shape, lambda i: (0,) * a.ndim)
    out_shape = [
        jax.ShapeDtypeStruct((n, D_ATTN), F32),
        jax.ShapeDtypeStruct((n, D_ATTN), F32),
        jax.ShapeDtypeStruct((n, D_ATTN), F32),
        jax.ShapeDtypeStruct((N_HEADS, n), F32),
        jax.ShapeDtypeStruct((n, D_CONV), BF16),
        jax.ShapeDtypeStruct((n, D_CONV), F32),
    ]
    return pl.pallas_call(
        _sample_proj_kernel,
        grid=(1,),
        in_specs=[full(a) for a in args],
        out_specs=[full(s) for s in out_shape],
        out_shape=out_shape,
        compiler_params=pltpu.CompilerParams(
            dimension_semantics=("arbitrary",), vmem_limit_bytes=VMEM_LIMIT),
        name="sample_proj",
    )(*args)


def _head_rows(h):
    return slice(h * HEAD_DIM, (h + 1) * HEAD_DIM)


def _paged_logits(h, qb_ref, k_refs, s_ref):
    qh = qb_ref[_head_rows(h), :]
    for j, k in enumerate(k_refs):
        prod = k[_head_rows(h), :] * qh
        s_ref[h:h + 1, j * LANES:(j + 1) * LANES] = jnp.sum(prod, axis=0, keepdims=True)


def _paged_softmax(s_ref, p_ref, a_ref, lf_refs, tri_ref, m_ref, l_ref, carry_ref):
    pages = len(lf_refs)
    m = m_ref[...]
    carry = carry_ref[...]
    lf = jnp.concatenate([r[...] for r in lf_refs], axis=0)
    n = pages * N_HEADS
    r = _dot(jnp.concatenate(_split3(lf), axis=0), tri_ref[...])
    r = r[0:n] + r[n:2 * n] + r[2 * n:3 * n]
    cs = r[:, 0:LANES]
    tot = r[:, LANES:2 * LANES]
    c_pages = []
    for j in range(pages):
        c_pages.append(cs[j * N_HEADS:(j + 1) * N_HEADS] + carry)
        carry = carry + tot[j * N_HEADS:(j + 1) * N_HEADS]
    s = s_ref[...] - jnp.concatenate(c_pages, axis=1)
    m_new = jnp.maximum(m, jnp.max(s, axis=-1, keepdims=True))
    alpha = jnp.exp(m - m_new)
    p = jnp.exp(s - m_new)
    p_ref[...] = p
    a_ref[...] = alpha
    m_ref[...] = m_new
    l_ref[...] = alpha * l_ref[...] + jnp.sum(p, axis=-1, keepdims=True)
    carry_ref[...] = carry


def _paged_values(h, p_ref, a_ref, v_refs, acc_ref):
    acc = acc_ref[_head_rows(h), :] * a_ref[h:h + 1, :]
    for j, v in enumerate(v_refs):
        acc = acc + v[_head_rows(h), :] * p_ref[h:h + 1, j * LANES:(j + 1) * LANES]
    acc_ref[_head_rows(h), :] = acc


def _paged_finish(qc_ref, knc_ref, vnc_ref, lfn_ref, m_ref, l_ref, acc_ref, carry_ref, o_ref):
    m = m_ref[...]
    qk = qc_ref[0] * knc_ref[0]
    s_new = jnp.concatenate([jnp.sum(qk[_head_rows(h)], axis=0, keepdims=True)
                             for h in range(N_HEADS)], axis=0)
    s_new = s_new - (carry_ref[:, 0:1] + lfn_ref[0])
    m_fin = jnp.maximum(m, s_new)
    alpha = jnp.exp(m - m_fin)
    p_new = jnp.exp(s_new - m_fin)
    l_fin = alpha * l_ref[...] + p_new
    for h in range(N_HEADS):
        rows = _head_rows(h)
        num = (alpha[h:h + 1, :] * jnp.sum(acc_ref[rows, :], axis=-1, keepdims=True)
               + p_new[h:h + 1, :] * vnc_ref[0, rows, :])
        o_ref[0, rows, :] = num / l_fin[h:h + 1, :]


def _page_copies(layer, pt_ref, seq, quarter, slot, caches, bufs, sems):
    pages = bufs[0].shape[1]
    copies = []
    for j in range(pages):
        page = pt_ref[seq, quarter * pages + j]
        for kind, (cache, buf) in enumerate(zip(caches, bufs)):
            copies.append(pltpu.make_async_copy(cache.at[layer, page], _page_view(buf, slot, j),
                                                sems.at[slot, kind]))
    return copies


def _start_pages(copies):
    kinds = 3
    for i, c in enumerate(copies):
        c.start(priority=1 if i % kinds == 1 else 0)


def _page_view(buf, slot, j):
    return buf.at[slot, j, pl.ds(0, buf.shape[2] - PAGE_SKEW_ROWS)]


def _ffn_paged_kernel(alpha, layer, pt_ref, x_ref, wgu_ref, wd_ref, g_ref, b_ref,
                      q_ref, kn_ref, vn_ref, lfn_ref, tri_ref, ck_ref, cv_ref, clf_ref,
                      y_ref, o_ref, act_ref, part_ref, qb_ref, s_ref, p_ref, a_ref,
                      m_ref, l_ref, acc_ref, carry_ref, kbuf, vbuf, lfbuf, sems):
    seq = pl.program_id(0)
    quarter = pl.program_id(1)
    n_seq = pl.num_programs(0)
    pages = kbuf.shape[1]
    caches, bufs = (ck_ref, cv_ref, clf_ref), (kbuf, vbuf, lfbuf)
    copies = functools.partial(_page_copies, layer, pt_ref, caches=caches, bufs=bufs, sems=sems)
    subs = _row_subtiles(x_ref.shape[0])

    def gate_up(c0, c1):
        def piece(rows, c):
            xb = x_ref[rows].astype(BF16)
            gate = _dot(xb, wgu_ref[:, c:c + FFN_COLS])
            up = _dot(xb, wgu_ref[:, D_FF + c:D_FF + c + FFN_COLS])
            act_ref[rows, c:c + FFN_COLS] = (jax.nn.silu(gate) * up).astype(BF16)
        return [functools.partial(piece, rows, c)
                for rows in subs for c in range(c0, c1, FFN_COLS)]

    def down(k0, k1, first=False, last=False):
        def piece(rows):
            part = _dot(act_ref[rows, k0:k1], wd_ref[k0:k1, :])
            if first:
                part_ref[rows] = part
            elif not last:
                part_ref[rows] += part
            else:
                y_ref[rows] = _layer_norm(alpha * x_ref[rows] + (part_ref[rows] + part),
                                          g_ref[...], b_ref[...])
        return [functools.partial(piece, rows) for rows in subs]

    c2, c4 = 4 * FFN_COLS, 8 * FFN_COLS
    ffn_work = [
        gate_up(0, c2),
        gate_up(c2, c4),
        gate_up(c4, D_FF) + down(0, c2, first=True),
        down(c2, c4) + down(c4, D_FF, last=True),
    ]
    last = len(ffn_work) - 1
    for k, pieces in enumerate(ffn_work):
        @pl.when(quarter == k)
        def _(k=k, pieces=pieces):
            slot = k % 2
            if k == 0:
                @pl.when(seq == 0)
                def _():
                    _start_pages(copies(seq, k, slot))
            if k < last:
                _start_pages(copies(seq, k + 1, 1 - slot))
            else:
                @pl.when(seq + 1 < n_seq)
                def _():
                    _start_pages(copies(seq + 1, 0, 1 - slot))
            for c in copies(seq, k, slot):
                c.wait()
            k_refs = [_page_view(kbuf, slot, j) for j in range(pages)]
            v_refs = [_page_view(vbuf, slot, j) for j in range(pages)]
            lf_refs = [_page_view(lfbuf, slot, j) for j in range(pages)]
            attn_work = (
                [functools.partial(_paged_logits, h, qb_ref, k_refs, s_ref)
                 for h in range(N_HEADS)]
                + [functools.partial(_paged_softmax, s_ref, p_ref, a_ref, lf_refs, tri_ref,
                                     m_ref, l_ref, carry_ref)]
                + [functools.partial(_paged_values, h, p_ref, a_ref, v_refs, acc_ref)
                   for h in range(N_HEADS)])
            if k == 0:
                m_ref[...] = jnp.full_like(m_ref, -jnp.inf)
                l_ref[...] = jnp.zeros_like(l_ref)
                acc_ref[...] = jnp.zeros_like(acc_ref)
                carry_ref[...] = jnp.zeros_like(carry_ref)
                qb_ref[...] = jnp.broadcast_to(q_ref[0], qb_ref.shape)
            for piece in attn_work + pieces:
                piece()
            if k == last:
                _paged_finish(q_ref, kn_ref, vn_ref, lfn_ref, m_ref, l_ref, acc_ref, carry_ref,
                              o_ref)


def _ffn_paged(layer, page_table, x, wgu, wd, g, b, alpha, q, kn, vn, lfn, tri,
               cache_kt, cache_vt, cache_lft):
    n, n_pages = page_table.shape
    quarters = 4
    assert n_pages % quarters == 0 and x.shape[0] % n == 0
    pages = n_pages // quarters
    rows = x.shape[0] // n
    assert rows % SUBLANES == 0
    x_spec = pl.BlockSpec((rows, D_MODEL), lambda s, k, pt: (s, 0))
    row = pl.BlockSpec((1, D_ATTN, 1), lambda s, k, pt: (s, 0, 0))

    hbm = pl.BlockSpec(memory_space=pl.ANY)
    in_specs = [x_spec, _resident(wgu, layer), _resident(wd, layer), _resident(g), _resident(b),
                row, row, row,
                pl.BlockSpec((1, N_HEADS, 1), lambda s, k, pt: (s, 0, 0)),
                _resident(tri), hbm, hbm, hbm]
    grid_spec = pltpu.PrefetchScalarGridSpec(
        num_scalar_prefetch=1,
        grid=(n, quarters),
        in_specs=in_specs,
        out_specs=[x_spec, row],
        scratch_shapes=[
            pltpu.VMEM((rows, D_FF), BF16),
            pltpu.VMEM((rows, D_MODEL), F32),
            pltpu.VMEM((D_ATTN, LANES), F32),
            pltpu.VMEM((N_HEADS, pages * PAGE_SIZE), F32),
            pltpu.VMEM((N_HEADS, pages * PAGE_SIZE), F32),
            pltpu.VMEM((N_HEADS, 1), F32),
            pltpu.VMEM((N_HEADS, 1), F32),
            pltpu.VMEM((N_HEADS, 1), F32),
            pltpu.VMEM((D_ATTN, LANES), F32),
            pltpu.VMEM((N_HEADS, LANES), F32),
            pltpu.VMEM((2, pages, D_ATTN + PAGE_SKEW_ROWS, PAGE_SIZE), F32),
            pltpu.VMEM((2, pages, D_ATTN + PAGE_SKEW_ROWS, PAGE_SIZE), F32),
            pltpu.VMEM((2, pages, N_HEADS + PAGE_SKEW_ROWS, PAGE_SIZE), F32),
            pltpu.SemaphoreType.DMA((2, 3)),
        ],
    )
    return pl.pallas_call(
        functools.partial(_ffn_paged_kernel, alpha, layer),
        grid_spec=grid_spec,
        out_shape=[jax.ShapeDtypeStruct(x.shape, F32),
                   jax.ShapeDtypeStruct((n, D_ATTN, 1), F32)],
        compiler_params=pltpu.CompilerParams(
            dimension_semantics=("arbitrary", "arbitrary"), vmem_limit_bytes=VMEM_LIMIT),
        name="ffn_paged",
    )(page_table, x, wgu, wd, g, b, q, kn, vn, lfn, tri, cache_kt, cache_vt, cache_lft)


def kernel(x_prompt, x_sample, cache_k, cache_v, cache_logf, state_conv, page_table, w_in, b_f, b_gate, conv_w, w_attn_proj, w_conv_proj, w_out, ln1_g, ln1_b, w_gate_up, w_down, ln2_g, ln2_b):
    batch, seq, _ = x_prompt.shape
    n_dec = x_sample.shape[0]
    depth = w_in.shape[0]
    n_pool = cache_k.shape[1]
    alpha = (2 * depth) ** 0.25

    cache_kt = jnp.transpose(cache_k, (0, 1, 3, 4, 2)).reshape(depth, n_pool, D_ATTN, PAGE_SIZE)
    cache_vt = jnp.transpose(cache_v, (0, 1, 3, 4, 2)).reshape(depth, n_pool, D_ATTN, PAGE_SIZE)
    cache_lft = jnp.transpose(cache_logf, (0, 1, 3, 2))
    tri = jnp.triu(jnp.ones((LANES, LANES), F32)).astype(BF16)
    tri_ones = jnp.concatenate([tri, jnp.ones((LANES, LANES), BF16)], axis=1)

    xp = x_prompt.reshape(batch * seq, D_MODEL)
    xs = x_sample.reshape(n_dec, D_MODEL)
    kv_all, lp, cp, ksm, vsm, lsm, csm = (), [], [], [], [], [], []
    wpa, wpc, wo = w_attn_proj.astype(BF16), w_conv_proj.astype(BF16), w_out.astype(BF16)
    wgu, wd = w_gate_up.astype(BF16), w_down.astype(BF16)
    for l in range(depth):
        wt = jnp.transpose(w_in[l])
        wn = jnp.concatenate([w_in[l][:, OFF_Q:OFF_K], w_in[l][:, OFF_H:OFF_G]], axis=1).astype(BF16)
        wkvt = wt[OFF_K:OFF_F].astype(BF16)
        wft = wt[OFF_F:OFF_H].astype(BF16)
        wg = w_in[l][:, OFF_G:].astype(BF16)
        bf = b_f[l].reshape(N_HEADS, 1)
        bg = b_gate[l].reshape(1, 2 * D_MODEL)
        cw = conv_w[l]
        g1, b1 = ln1_g[l].reshape(1, D_MODEL), ln1_b[l].reshape(1, D_MODEL)
        g2, b2 = ln2_g[l].reshape(1, D_MODEL), ln2_b[l].reshape(1, D_MODEL)

        q, kt, vt, lf, c, cb, st = _prompt_proj(l, depth, xp, wn, wkvt, wft, bf, cw, tri,
                                                batch, seq, kv_all)
        kv_all = (kt, vt)
        a = _prompt_attn(l, q, kt, vt, c, batch, seq)
        x1 = _mix(l, xp, a, cb, wg, bg, wpa, wpc, wo, g1, b1, alpha, MIX_ROWS)
        lp.append(lf); cp.append(st)
        h0, h1 = state_conv[l, :, 0, :], state_conv[l, :, 1, :]
        qs, ks, vs, lfs, cbs, us = _sample_proj(xs, wn, wkvt, wft, bf, cw, h0, h1)

        xp, a_s = _ffn_paged(l, page_table, x1, wgu, wd, g2, b2, alpha,
                             qs.reshape(n_dec, D_ATTN, 1), ks.reshape(n_dec, D_ATTN, 1),
                             vs.reshape(n_dec, D_ATTN, 1),
                             jnp.transpose(lfs).reshape(n_dec, N_HEADS, 1),
                             tri_ones, cache_kt, cache_vt, cache_lft)

        x1s = _mix(l, xs, a_s.reshape(n_dec, D_ATTN).astype(BF16), cbs, wg, bg, wpa, wpc, wo,
                   g1, b1, alpha, n_dec)
        xs = _ffn(l, x1s, wgu, wd, g2, b2, alpha, n_dec)
        ksm.append(ks); vsm.append(vs); lsm.append(lfs); csm.append(jnp.stack([h1, us], axis=1))

    def heads_last(t):
        return jnp.transpose(t.reshape(depth, batch, N_HEADS, HEAD_DIM, seq), (0, 1, 4, 2, 3))

    y_prompt = xp.reshape(batch, seq, D_MODEL)
    y_sample = xs.reshape(n_dec, 1, D_MODEL)
    k_prompt = heads_last(kv_all[0])
    v_prompt = heads_last(kv_all[1])
    logf_prompt = jnp.transpose(jnp.stack(lp), (0, 1, 3, 2))
    conv_prompt = jnp.stack(cp)
    k_sample = jnp.stack(ksm).reshape(depth, n_dec, 1, N_HEADS, HEAD_DIM)
    v_sample = jnp.stack(vsm).reshape(depth, n_dec, 1, N_HEADS, HEAD_DIM)
    logf_sample = jnp.transpose(jnp.stack(lsm), (0, 2, 1)).reshape(depth, n_dec, 1, N_HEADS)
    conv_sample = jnp.stack(csm)
    return (y_prompt, y_sample, k_prompt, v_prompt, logf_prompt, conv_prompt,
            k_sample, v_sample, logf_sample, conv_sample)
```
